```python
import math
import jax
import jax.numpy as jnp
from jax import lax
import numpy as np

D_MODEL = 1024
BATCH = 8
SEQ = 2048
DEPTH = 2
DEC_BATCH = 128
DEC_SEQ = 8
PAST_LEN = 16384
PAGE_SIZE = 128

MIX_WIDTH = D_MODEL
HGRN_WIDTH = MIX_WIDTH // 2
CONV_WIDTH = MIX_WIDTH - HGRN_WIDTH
HGRN_HEAD_DIM = 128
HGRN_HEADS = HGRN_WIDTH // HGRN_HEAD_DIM
CONV_K = 3
D_FF = 4 * D_MODEL
CHUNK = 64
PROJ_WIDTH = 4 * HGRN_WIDTH + 3 * CONV_WIDTH
SPLITS = (HGRN_WIDTH, 2 * HGRN_WIDTH, 3 * HGRN_WIDTH, 4 * HGRN_WIDTH,
          4 * HGRN_WIDTH + CONV_WIDTH, 4 * HGRN_WIDTH + 2 * CONV_WIDTH)
ALPHA = float((2 * DEPTH) ** 0.25)
BETA = float((8 * DEPTH) ** -0.25)
LN_EPS = 1e-5
RMS_EPS = 1e-6

kernel_name = "hgrn2_shortconv_hybrid_step"


def _layer_norm(x, g, b):
    xf = x.astype(jnp.float32)
    mu = jnp.mean(xf, axis=-1, keepdims=True)
    xc = xf - mu
    var = jnp.mean(xc * xc, axis=-1, keepdims=True)
    y = xc * lax.rsqrt(var + LN_EPS) * g.astype(jnp.float32) + b.astype(jnp.float32)
    return y.astype(x.dtype)


def _hgrn2(q, log_f, k, v, s0):
    bsz, t_len, n_h, d_k = q.shape
    d_v = v.shape[-1]
    c = math.gcd(t_len, CHUNK)
    n = t_len // c

    def to_chunks(a):
        return jnp.moveaxis(a.reshape(bsz, n, c, *a.shape[2:]), 1, 0)

    causal = jnp.tril(jnp.ones((c, c), dtype=bool))[None, :, :, None, None]

    def step(s_prev, inp):
        qc, lfc, kc, vc = inp
        b = jnp.cumsum(lfc, axis=1)
        o_inter = jnp.einsum('bthk,bhkv->bthv', qc * jnp.exp(b), s_prev)
        diff = b[:, :, None] - b[:, None, :]
        decay = jnp.exp(jnp.where(causal, diff, -jnp.inf))
        scores = jnp.einsum('bthk,btshk,bshk->bhts', qc, decay, kc)
        o_intra = jnp.einsum('bhts,bshv->bthv', scores, vc)
        b_last = b[:, -1]
        k_dec = kc * jnp.exp(b_last[:, None] - b)
        s_new = jnp.exp(b_last)[..., None] * s_prev + jnp.einsum('bshk,bshv->bhkv', k_dec, vc)
        return s_new, o_inter + o_intra

    s_fin, o = lax.scan(step, s0, (to_chunks(q), to_chunks(log_f), to_chunks(k), to_chunks(v)))
    o = jnp.moveaxis(o, 0, 1).reshape(bsz, t_len, n_h, d_v)
    return o, s_fin


def _layer(x, s_hgrn, conv_buf, lb, w_in, conv_w, onorm_g, w_out,
           ln1_g, ln1_b, w_ff1, w_ff2, ln2_g, ln2_b):
    bsz, t_len, _ = x.shape
    f32 = jnp.float32
    proj = jnp.einsum('btd,dp->btp', x, w_in)
    q, f_pre, i_val, g, gate_b, gate_c, h = jnp.split(proj, SPLITS, axis=-1)

    shp = (bsz, t_len, HGRN_HEADS, HGRN_HEAD_DIM)
    qh = jax.nn.silu(q.astype(f32)).reshape(shp)
    lbh = lb.reshape(HGRN_HEADS, HGRN_HEAD_DIM)
    log_f = jnp.logaddexp(jnp.log(lbh), jnp.log1p(-lbh) + jax.nn.log_sigmoid(f_pre.astype(f32).reshape(shp)))
    kh = -jnp.expm1(log_f)
    vh = i_val.astype(f32).reshape(shp)
    o, s_new = _hgrn2(qh, log_f, kh, vh, s_hgrn.astype(f32))
    o = o * lax.rsqrt(jnp.mean(o * o, axis=-1, keepdims=True) + RMS_EPS)
    o = o.reshape(bsz, t_len, HGRN_WIDTH) * onorm_g.astype(f32) * jax.nn.silu(g.astype(f32))

    u = gate_c * h
    full = jnp.concatenate([conv_buf.astype(u.dtype), u], axis=1)
    conv_out = sum(conv_w[j] * full[:, j:j + t_len] for j in range(CONV_K))
    yc = gate_b * conv_out
    new_buf = full[:, t_len:]

    mix = jnp.einsum('btm,md->btd', jnp.concatenate([o.astype(x.dtype), yc], axis=-1), w_out)
    x = _layer_norm(ALPHA * x + mix, ln1_g, ln1_b)

    hid = jnp.square(jax.nn.relu(jnp.einsum('btd,df->btf', x, w_ff1)))
    x = _layer_norm(ALPHA * x + jnp.einsum('btf,fd->btd', hid, w_ff2), ln2_g, ln2_b)
    return x, s_new, new_buf


def setup_inputs(seed: int = 0) -> dict:
    key = jax.random.key(seed)
    ks = jax.random.split(key, 16)
    nrm = jax.random.normal
    x_prompt = nrm(ks[0], (BATCH, SEQ, D_MODEL), jnp.float32)
    x_sample = nrm(ks[1], (DEC_BATCH, DEC_SEQ, D_MODEL), jnp.float32)
    state_hgrn = 0.5 * nrm(ks[2], (DEPTH, DEC_BATCH, HGRN_HEADS, HGRN_HEAD_DIM, HGRN_HEAD_DIM), jnp.float32)
    state_conv = 0.5 * nrm(ks[3], (DEPTH, DEC_BATCH, CONV_K - 1, CONV_WIDTH), jnp.float32)
    col_scale = jnp.concatenate([
        jnp.ones((2 * HGRN_WIDTH,), jnp.float32), jnp.full((HGRN_WIDTH,), BETA, jnp.float32),
        jnp.ones((HGRN_WIDTH + 2 * CONV_WIDTH,), jnp.float32), jnp.full((CONV_WIDTH,), BETA, jnp.float32)])
    w_in = nrm(ks[4], (DEPTH, D_MODEL, PROJ_WIDTH), jnp.float32) * (D_MODEL ** -0.5) * col_scale
    lb_logits = nrm(ks[5], (DEPTH, HGRN_WIDTH), jnp.float32)
    conv_w = nrm(ks[6], (DEPTH, CONV_K, CONV_WIDTH), jnp.float32) * (CONV_K ** -0.5)
    onorm_g = 1.0 + 0.01 * nrm(ks[7], (DEPTH, HGRN_WIDTH), jnp.float32)
    w_out = nrm(ks[8], (DEPTH, MIX_WIDTH, D_MODEL), jnp.float32) * (MIX_WIDTH ** -0.5) * BETA
    ln1_g = 1.0 + 0.01 * nrm(ks[9], (DEPTH, D_MODEL), jnp.float32)
    ln1_b = 0.01 * nrm(ks[10], (DEPTH, D_MODEL), jnp.float32)
    w_ff1 = nrm(ks[11], (DEPTH, D_MODEL, D_FF), jnp.float32) * (D_MODEL ** -0.5) * BETA
    w_ff2 = nrm(ks[12], (DEPTH, D_FF, D_MODEL), jnp.float32) * (D_FF ** -0.5) * BETA
    ln2_g = 1.0 + 0.01 * nrm(ks[13], (DEPTH, D_MODEL), jnp.float32)
    ln2_b = 0.01 * nrm(ks[14], (DEPTH, D_MODEL), jnp.float32)
    return {"x_prompt": x_prompt, "x_sample": x_sample,
            "state_hgrn": state_hgrn, "state_conv": state_conv,
            "w_in": w_in, "lb_logits": lb_logits, "conv_w": conv_w, "onorm_g": onorm_g,
            "w_out": w_out, "ln1_g": ln1_g, "ln1_b": ln1_b,
            "w_ff1": w_ff1, "w_ff2": w_ff2, "ln2_g": ln2_g, "ln2_b": ln2_b}


def reference(x_prompt, x_sample, state_hgrn, state_conv, w_in, lb_logits, conv_w, onorm_g,
              w_out, ln1_g, ln1_b, w_ff1, w_ff2, ln2_g, ln2_b):
    p = jax.nn.softmax(lb_logits.astype(jnp.float32), axis=0)
    cum = jnp.cumsum(p, axis=0)
    lower_bounds = cum - cum[0:1]

    yp, ys = x_prompt, x_sample
    hp_list, cp_list, hs_list, cs_list = [], [], [], []
    for l in range(DEPTH):
        weights = (w_in[l], conv_w[l], onorm_g[l], w_out[l], ln1_g[l], ln1_b[l],
                   w_ff1[l], w_ff2[l], ln2_g[l], ln2_b[l])
        s0 = jnp.zeros((BATCH, HGRN_HEADS, HGRN_HEAD_DIM, HGRN_HEAD_DIM), jnp.float32)
        b0 = jnp.zeros((BATCH, CONV_K - 1, CONV_WIDTH), x_prompt.dtype)
        yp, hp, cp = _layer(yp, s0, b0, lower_bounds[l], *weights)
        ys, hs, cs = _layer(ys, state_hgrn[l], state_conv[l], lower_bounds[l], *weights)
        hp_list.append(hp.astype(x_prompt.dtype))
        cp_list.append(cp.astype(x_prompt.dtype))
        hs_list.append(hs.astype(state_hgrn.dtype))
        cs_list.append(cs.astype(state_conv.dtype))
    new_hgrn_prompt = jnp.stack(hp_list, axis=0)
    new_conv_prompt = jnp.stack(cp_list, axis=0)
    new_hgrn_sample = jnp.stack(hs_list, axis=0)
    new_conv_sample = jnp.stack(cs_list, axis=0)
    return (yp, ys, new_hgrn_prompt, new_conv_prompt, new_hgrn_sample, new_conv_sample)
```

```python
import functools
import math

import numpy as np
import jax
import jax.numpy as jnp
from jax import lax
from jax.experimental import pallas as pl
from jax.experimental.pallas import tpu as pltpu

F32 = jnp.float32
BF16 = jnp.bfloat16

D_MODEL = 1024
DEPTH = 2
HGRN_WIDTH = 512
CONV_WIDTH = 512
HEAD_DIM = 128
N_HEADS = HGRN_WIDTH // HEAD_DIM
CONV_K = 3
D_FF = 4 * D_MODEL
PROJ_WIDTH = 4 * HGRN_WIDTH + 3 * CONV_WIDTH
FF_CHUNK = 1024
ALPHA = float((2 * DEPTH) ** 0.25)
LN_EPS = 1e-5
RMS_EPS = 1e-6

OFF_Q, OFF_F, OFF_I, OFF_G = 0, HGRN_WIDTH, 2 * HGRN_WIDTH, 3 * HGRN_WIDTH
OFF_B = 4 * HGRN_WIDTH
OFF_C = OFF_B + CONV_WIDTH
OFF_H = OFF_C + CONV_WIDTH

PROMPT_TILE = 256
SAMPLE_SEQS = 16
V7X_VMEM_LIMIT = 58 * 1024 * 1024

ROW_LN1G, ROW_LN1B, ROW_LN2G, ROW_LN2B, ROW_LOGLB, ROW_GATE, ROW_CW01, ROW_CW2 = range(8)


def _dot(a, b):
    return jnp.dot(a, b, preferred_element_type=F32)


def _dot_nt(a, b):
    return lax.dot_general(a, b, (((1,), (1,)), ((), ())), preferred_element_type=F32)


def _dot_tn(a, b):
    return lax.dot_general(a, b, (((0,), (0,)), ((), ())), preferred_element_type=F32)


def _sigmoid(x):
    return 1.0 / (1.0 + jnp.exp(-x))


def _layer_norm(y, g, b):
    mu = jnp.mean(y, axis=-1, keepdims=True)
    yc = y - mu
    var = jnp.mean(yc * yc, axis=-1, keepdims=True)
    return yc * lax.rsqrt(var + LN_EPS) * g + b


def _front(x, win_ref, vec_ref, tri_ref, proj_scr, q_scr, kk_scr, lf_scr, b_scr):
    proj_scr[...] = _dot(x.astype(BF16), win_ref[...])
    qp = proj_scr[:, OFF_Q:OFF_Q + HGRN_WIDTH]
    q_scr[...] = qp * _sigmoid(qp)
    z = proj_scr[:, OFF_F:OFF_F + HGRN_WIDTH]
    log_lb = vec_ref[ROW_LOGLB:ROW_LOGLB + 1, 0:HGRN_WIDTH]
    log_1mlb = vec_ref[ROW_LOGLB:ROW_LOGLB + 1, HGRN_WIDTH:2 * HGRN_WIDTH]
    one_m_lb = vec_ref[ROW_GATE:ROW_GATE + 1, 0:HGRN_WIDTH]
    e = jnp.exp(-jnp.abs(z))
    log_sig = jnp.minimum(z, 0.0) - jnp.log1p(e)
    c = log_1mlb + log_sig
    lf = jnp.maximum(log_lb, c) + jnp.log1p(jnp.exp(-jnp.abs(log_lb - c)))
    lf_scr[...] = lf
    kk_scr[...] = one_m_lb * jnp.where(z >= 0, e, 1.0) / (1.0 + e)
    hi = lf.astype(BF16)
    r1 = lf - hi.astype(F32)
    mid = r1.astype(BF16)
    lo = (r1 - mid.astype(F32)).astype(BF16)
    tri = tri_ref[...]
    b_scr[...] = _dot(tri, hi) + _dot(tri, mid) + _dot(tri, lo)


def _intra_head(h, seg, tm, proj_scr, q_scr, kk_scr, lf_scr, b_scr, lv_ref):
    hs = slice(h * HEAD_DIM, (h + 1) * HEAD_DIM)
    qh = q_scr[:, hs]
    kh = kk_scr[:, hs]
    lfh = lf_scr[:, hs]
    bh = b_scr[:, hs]
    vh = proj_scr[:, OFF_I + h * HEAD_DIM:OFF_I + (h + 1) * HEAD_DIM]
    row = lax.broadcasted_iota(jnp.int32, (tm, HEAD_DIM), 0)
    lv = lv_ref[...]
    sc = jnp.zeros((tm, tm), F32)
    half = seg // 2
    while half >= 1:
        lg = int(math.log2(half))
        first = ((row >> lg) & 1) == 0
        if half >= 4:
            blk = 2 * half
            pieces = [jnp.broadcast_to(b_scr[pl.ds(blk * jb + half - 1, 1), hs], (blk, HEAD_DIM))
                      for jb in range(tm // blk)]
            bm = pieces[0] if len(pieces) == 1 else jnp.concatenate(pieces, axis=0)
            arg = jnp.where(first, bm - bh, bh - bm)
        elif half == 2:
            r = row & 3
            lf_next = pltpu.roll(lfh, tm - 1, 0)
            lf_prev = pltpu.roll(lfh, 1, 0)
            arg = jnp.where(r == 0, lf_next, jnp.where(r == 1, 0.0, jnp.where(r == 2, lfh, lfh + lf_prev)))
        else:
            arg = jnp.where(first, 0.0, lfh)
        xl = (jnp.where(first, kh, qh) * jnp.exp(arg)).astype(BF16)
        sc = jnp.where(lv == lg, _dot_nt(xl, xl), sc)
        half //= 2
    dg = jnp.sum(qh * kh, axis=-1, keepdims=True)
    return _dot(sc.astype(BF16), vh.astype(BF16)) + dg * vh


def _finish_head(h, o, proj_scr, vec_ref, o_scr):
    hs = slice(h * HEAD_DIM, (h + 1) * HEAD_DIM)
    g = proj_scr[:, OFF_G + h * HEAD_DIM:OFF_G + (h + 1) * HEAD_DIM]
    onorm = vec_ref[ROW_GATE:ROW_GATE + 1, HGRN_WIDTH + h * HEAD_DIM:HGRN_WIDTH + (h + 1) * HEAD_DIM]
    o = o * lax.rsqrt(jnp.mean(o * o, axis=-1, keepdims=True) + RMS_EPS)
    o_scr[:, hs] = o * onorm * (g * _sigmoid(g))


def _back(x, mixin, wout_ref, w1_ref, w2_ref, vec_ref):
    mix = _dot(mixin, wout_ref[...])
    x1 = _layer_norm(ALPHA * x + mix, vec_ref[ROW_LN1G:ROW_LN1G + 1, :], vec_ref[ROW_LN1B:ROW_LN1B + 1, :])
    x1b = x1.astype(BF16)
    acc = jnp.zeros(x.shape, F32)
    for c in range(D_FF // FF_CHUNK):
        cs = slice(c * FF_CHUNK, (c + 1) * FF_CHUNK)
        hid = jnp.maximum(_dot(x1b, w1_ref[:, cs]), 0.0)
        acc = acc + _dot((hid * hid).astype(BF16), w2_ref[cs, :])
    return _layer_norm(ALPHA * x1 + acc, vec_ref[ROW_LN2G:ROW_LN2G + 1, :], vec_ref[ROW_LN2B:ROW_LN2B + 1, :])


def _conv_weights(vec_ref):
    w0 = vec_ref[ROW_CW01:ROW_CW01 + 1, 0:CONV_WIDTH]
    w1 = vec_ref[ROW_CW01:ROW_CW01 + 1, CONV_WIDTH:2 * CONV_WIDTH]
    w2 = vec_ref[ROW_CW2:ROW_CW2 + 1, 0:CONV_WIDTH]
    return w0, w1, w2


def _prompt_kernel(x_ref, win_ref, wout_ref, w1_ref, w2_ref, vec_ref, lv_ref, tri_ref,
                   y_ref, hst_ref, cst_ref,
                   proj_scr, q_scr, kk_scr, lf_scr, b_scr, u_scr, o_scr):
    tm = PROMPT_TILE
    j = pl.program_id(1)

    @pl.when(j == 0)
    def _():
        hst_ref[...] = jnp.zeros(hst_ref.shape, F32)
        u_scr[0:8, :] = jnp.zeros((8, CONV_WIDTH), F32)

    x = x_ref[0]
    _front(x, win_ref, vec_ref, tri_ref, proj_scr, q_scr, kk_scr, lf_scr, b_scr)

    for h in range(N_HEADS):
        hs = slice(h * HEAD_DIM, (h + 1) * HEAD_DIM)
        o = _intra_head(h, tm, tm, proj_scr, q_scr, kk_scr, lf_scr, b_scr, lv_ref)
        qh = q_scr[:, hs]
        kh = kk_scr[:, hs]
        bh = b_scr[:, hs]
        vh = proj_scr[:, OFF_I + h * HEAD_DIM:OFF_I + (h + 1) * HEAD_DIM]
        s_prev = hst_ref[0, h]
        o = o + _dot((qh * jnp.exp(bh)).astype(BF16), s_prev.astype(BF16))
        b_last = b_scr[tm - 1:tm, hs]
        kd = (kh * jnp.exp(b_last - bh)).astype(BF16)
        upd = _dot_tn(kd, vh.astype(BF16))
        a_col = jnp.exp(jnp.transpose(b_scr[tm - 8:tm, hs])[:, 7:8])
        hst_ref[0, h] = a_col * s_prev + upd
        _finish_head(h, o, proj_scr, vec_ref, o_scr)

    u = proj_scr[:, OFF_C:OFF_C + CONV_WIDTH] * proj_scr[:, OFF_H:OFF_H + CONV_WIDTH]
    u_scr[8:8 + tm, :] = u
    w0, w1, w2 = _conv_weights(vec_ref)
    conv = w0 * u_scr[6:6 + tm, :] + w1 * u_scr[7:7 + tm, :] + w2 * u
    yc = proj_scr[:, OFF_B:OFF_B + CONV_WIDTH] * conv
    tail = u_scr[tm:tm + 8, :]
    u_scr[0:8, :] = tail

    @pl.when(j == pl.num_programs(1) - 1)
    def _():
        cst_ref[0] = tail[6:8, :]

    mixin = jnp.concatenate([o_scr[...], yc], axis=-1).astype(BF16)
    y_ref[0] = _back(x, mixin, wout_ref, w1_ref, w2_ref, vec_ref)


def _sample_kernel(x_ref, sin_ref, ext_ref, win_ref, wout_ref, w1_ref, w2_ref, vec_ref, lv_ref, tri_ref,
                   y_ref, sout_ref, u_ref,
                   proj_scr, q_scr, kk_scr, lf_scr, b_scr, o_scr, qe_scr, kd_scr, vb_scr):
    seg = 8
    tm = SAMPLE_SEQS * seg
    x = x_ref[...]
    _front(x, win_ref, vec_ref, tri_ref, proj_scr, q_scr, kk_scr, lf_scr, b_scr)

    for h in range(N_HEADS):
        hs = slice(h * HEAD_DIM, (h + 1) * HEAD_DIM)
        o_scr[:, hs] = _intra_head(h, seg, tm, proj_scr, q_scr, kk_scr, lf_scr, b_scr, lv_ref)
        bh = b_scr[:, hs]
        pieces = [jnp.broadcast_to(b_scr[pl.ds(seg * jb + seg - 1, 1), hs], (seg, HEAD_DIM))
                  for jb in range(tm // seg)]
        b_last = jnp.concatenate(pieces, axis=0)
        qe_scr[:, hs] = (q_scr[:, hs] * jnp.exp(bh)).astype(BF16)
        kd_scr[:, hs] = (kk_scr[:, hs] * jnp.exp(b_last - bh)).astype(BF16)
    vb_scr[...] = proj_scr[:, OFF_I:OFF_I + HGRN_WIDTH].astype(BF16)

    top = lax.broadcasted_iota(jnp.int32, (2 * seg, HEAD_DIM), 0) < seg

    def pair_body(p, carry):
        r0 = pl.multiple_of(p * 2 * seg, 2 * seg)
        for h in range(N_HEADS):
            hs = slice(h * HEAD_DIM, (h + 1) * HEAD_DIM)
            qe = qe_scr[pl.ds(r0, 2 * seg), hs]
            kd = kd_scr[pl.ds(r0, 2 * seg), hs]
            vv = vb_scr[pl.ds(r0, 2 * seg), hs]
            bt = jnp.transpose(b_scr[pl.ds(r0, 2 * seg), hs])
            s_a = sin_ref[2 * p, h]
            s_b = sin_ref[2 * p + 1, h]
            o_a = _dot(qe, s_a.astype(BF16))
            o_b = _dot(qe, s_b.astype(BF16))
            o_scr[pl.ds(r0, 2 * seg), hs] = o_scr[pl.ds(r0, 2 * seg), hs] + jnp.where(top, o_a, o_b)
            zero = jnp.zeros_like(kd)
            sout_ref[2 * p, h] = jnp.exp(bt[:, seg - 1:seg]) * s_a + _dot_tn(jnp.where(top, kd, zero), vv)
            sout_ref[2 * p + 1, h] = jnp.exp(bt[:, 2 * seg - 1:2 * seg]) * s_b + _dot_tn(jnp.where(top, zero, kd), vv)
        return carry

    lax.fori_loop(0, SAMPLE_SEQS // 2, pair_body, 0)

    for h in range(N_HEADS):
        hs = slice(h * HEAD_DIM, (h + 1) * HEAD_DIM)
        _finish_head(h, o_scr[:, hs], proj_scr, vec_ref, o_scr)

    u = proj_scr[:, OFF_C:OFF_C + CONV_WIDTH] * proj_scr[:, OFF_H:OFF_H + CONV_WIDTH]
    u_ref[...] = u
    ext = ext_ref[...]
    tmod = lax.broadcasted_iota(jnp.int32, (tm, CONV_WIDTH), 0) & (seg - 1)
    u1 = jnp.where(tmod == 0, pltpu.roll(ext, tm - 1, 0), pltpu.roll(u, 1, 0))
    u2 = jnp.where(tmod < 2, ext, pltpu.roll(u, 2, 0))
    w0, w1, w2 = _conv_weights(vec_ref)
    yc = proj_scr[:, OFF_B:OFF_B + CONV_WIDTH] * (w0 * u2 + w1 * u1 + w2 * u)

    mixin = jnp.concatenate([o_scr[...], yc], axis=-1).astype(BF16)
    y_ref[...] = _back(x, mixin, wout_ref, w1_ref, w2_ref, vec_ref)


def _level_table(tm, seg):
    t = np.arange(tm)[:, None]
    s = np.arange(tm)[None, :]
    x = np.maximum(t ^ s, 1)
    lv = np.floor(np.log2(x)).astype(np.int32)
    ok = (s < t) & ((t // seg) == (s // seg))
    return np.where(ok, lv, -1).astype(np.int32)


def _tri_table(tm, seg):
    t = np.arange(tm)[:, None]
    s = np.arange(tm)[None, :]
    return ((s <= t) & ((t // seg) == (s // seg))).astype(np.float32)


def _const_spec(shape):
    nd = len(shape)
    return pl.BlockSpec(shape, lambda *_: (0,) * nd, pipeline_mode=pl.Buffered(1))


def _weight_specs(layer):
    def wspec(k, n):
        return pl.BlockSpec((None, k, n), lambda *_: (layer, 0, 0), pipeline_mode=pl.Buffered(1))
    return [wspec(D_MODEL, PROJ_WIDTH), wspec(D_MODEL, D_MODEL), wspec(D_MODEL, D_FF), wspec(D_FF, D_MODEL),
            pl.BlockSpec((None, 8, D_MODEL), lambda *_: (layer, 0, 0), pipeline_mode=pl.Buffered(1))]


def _prompt_layer(layer, x, weights, vecs):
    bsz, seq, _ = x.shape
    tm = PROMPT_TILE
    lv = jnp.asarray(_level_table(tm, tm))
    tri = jnp.asarray(_tri_table(tm, tm), dtype=BF16)
    return pl.pallas_call(
        _prompt_kernel,
        grid=(bsz, seq // tm),
        in_specs=[pl.BlockSpec((1, tm, D_MODEL), lambda b, j: (b, j, 0))] + _weight_specs(layer)
        + [_const_spec((tm, tm)), _const_spec((tm, tm))],
        out_specs=[pl.BlockSpec((1, tm, D_MODEL), lambda b, j: (b, j, 0)),
                   pl.BlockSpec((1, N_HEADS, HEAD_DIM, HEAD_DIM), lambda b, j: (b, 0, 0, 0)),
                   pl.BlockSpec((1, CONV_K - 1, CONV_WIDTH), lambda b, j: (b, 0, 0))],
        out_shape=[jax.ShapeDtypeStruct((bsz, seq, D_MODEL), F32),
                   jax.ShapeDtypeStruct((bsz, N_HEADS, HEAD_DIM, HEAD_DIM), F32),
                   jax.ShapeDtypeStruct((bsz, CONV_K - 1, CONV_WIDTH), F32)],
        scratch_shapes=[pltpu.VMEM((tm, PROJ_WIDTH), F32),
                        pltpu.VMEM((tm, HGRN_WIDTH), F32), pltpu.VMEM((tm, HGRN_WIDTH), F32),
                        pltpu.VMEM((tm, HGRN_WIDTH), F32), pltpu.VMEM((tm, HGRN_WIDTH), F32),
                        pltpu.VMEM((tm + 8, CONV_WIDTH), F32), pltpu.VMEM((tm, HGRN_WIDTH), F32)],
        compiler_params=pltpu.CompilerParams(dimension_semantics=("arbitrary", "arbitrary"),
                                             vmem_limit_bytes=V7X_VMEM_LIMIT),
        name=f"prompt_layer{layer}",
    )(x, *weights, vecs, lv, tri)


def _sample_layer(layer, x, state, ext, weights, vecs):
    rows = x.shape[0]
    seg = 8
    tm = SAMPLE_SEQS * seg
    n_seq = rows // seg
    lv = jnp.asarray(_level_table(tm, seg))
    tri = jnp.asarray(_tri_table(tm, seg), dtype=BF16)
    st_spec = pl.BlockSpec((None, SAMPLE_SEQS, N_HEADS, HEAD_DIM, HEAD_DIM), lambda i: (layer, i, 0, 0, 0))
    return pl.pallas_call(
        _sample_kernel,
        grid=(n_seq // SAMPLE_SEQS,),
        in_specs=[pl.BlockSpec((tm, D_MODEL), lambda i: (i, 0)), st_spec,
                  pl.BlockSpec((None, tm, CONV_WIDTH), lambda i: (layer, i, 0))] + _weight_specs(layer)
        + [_const_spec((tm, tm)), _const_spec((tm, tm))],
        out_specs=[pl.BlockSpec((tm, D_MODEL), lambda i: (i, 0)),
                   pl.BlockSpec((SAMPLE_SEQS, N_HEADS, HEAD_DIM, HEAD_DIM), lambda i: (i, 0, 0, 0)),
                   pl.BlockSpec((tm, CONV_WIDTH), lambda i: (i, 0))],
        out_shape=[jax.ShapeDtypeStruct((rows, D_MODEL), F32),
                   jax.ShapeDtypeStruct((n_seq, N_HEADS, HEAD_DIM, HEAD_DIM), F32),
                   jax.ShapeDtypeStruct((rows, CONV_WIDTH), F32)],
        scratch_shapes=[pltpu.VMEM((tm, PROJ_WIDTH), F32),
                        pltpu.VMEM((tm, HGRN_WIDTH), F32), pltpu.VMEM((tm, HGRN_WIDTH), F32),
                        pltpu.VMEM((tm, HGRN_WIDTH), F32), pltpu.VMEM((tm, HGRN_WIDTH), F32),
                        pltpu.VMEM((tm, HGRN_WIDTH), F32),
                        pltpu.VMEM((tm, HGRN_WIDTH), BF16), pltpu.VMEM((tm, HGRN_WIDTH), BF16),
                        pltpu.VMEM((tm, HGRN_WIDTH), BF16)],
        compiler_params=pltpu.CompilerParams(dimension_semantics=("arbitrary",),
                                             vmem_limit_bytes=V7X_VMEM_LIMIT),
        name=f"sample_layer{layer}",
    )(x, state, ext, *weights, vecs, lv, tri)


def kernel(x_prompt, x_sample, state_hgrn, state_conv, w_in, lb_logits, conv_w, onorm_g, w_out,
           ln1_g, ln1_b, w_ff1, w_ff2, ln2_g, ln2_b):
    n_seq, dec_seq, _ = x_sample.shape
    p = jax.nn.softmax(lb_logits.astype(F32), axis=0)
    cum = jnp.cumsum(p, axis=0)
    lb = cum - cum[0:1]
    zeros_half = jnp.zeros((DEPTH, CONV_WIDTH), F32)
    vecs = jnp.stack([
        ln1_g, ln1_b, ln2_g, ln2_b,
        jnp.concatenate([jnp.log(lb), jnp.log1p(-lb)], axis=-1),
        jnp.concatenate([1.0 - lb, onorm_g], axis=-1),
        jnp.concatenate([conv_w[:, 0], conv_w[:, 1]], axis=-1),
        jnp.concatenate([conv_w[:, 2], zeros_half], axis=-1)], axis=1).astype(F32)
    weights = (w_in.astype(BF16), w_out.astype(BF16), w_ff1.astype(BF16), w_ff2.astype(BF16))
    ext = jnp.pad(state_conv, ((0, 0), (0, 0), (0, dec_seq - (CONV_K - 1)), (0, 0)))
    ext = ext.reshape(DEPTH, n_seq * dec_seq, CONV_WIDTH)

    yp = x_prompt
    ys = x_sample.reshape(n_seq * dec_seq, D_MODEL)
    hp, cp, hs_, cs = [], [], [], []
    for layer in range(DEPTH):
        yp, hgrn_p, conv_p = _prompt_layer(layer, yp, weights, vecs)
        ys, hgrn_s, u_s = _sample_layer(layer, ys, state_hgrn, ext, weights, vecs)
        hp.append(hgrn_p)
        cp.append(conv_p)
        hs_.append(hgrn_s)
        cs.append(u_s.reshape(n_seq, dec_seq, CONV_WIDTH)[:, dec_seq - (CONV_K - 1):])
    return (yp, ys.reshape(n_seq, dec_seq, D_MODEL), jnp.stack(hp), jnp.stack(cp), jnp.stack(hs_), jnp.stack(cs))
```

```python
import functools
import math

import numpy as np
import jax
import jax.numpy as jnp
from jax import lax
from jax.experimental import pallas as pl
from jax.experimental.pallas import tpu as pltpu

F32 = jnp.float32
BF16 = jnp.bfloat16

D_MODEL = 1024
DEPTH = 2
HGRN_WIDTH = 512
CONV_WIDTH = 512
HEAD_DIM = 128
N_HEADS = HGRN_WIDTH // HEAD_DIM
CONV_K = 3
D_FF = 4 * D_MODEL
PROJ_WIDTH = 4 * HGRN_WIDTH + 3 * CONV_WIDTH
FF_CHUNK = 1024
N_FF_CHUNKS = D_FF // FF_CHUNK
ALPHA = float((2 * DEPTH) ** 0.25)
LN_EPS = 1e-5
RMS_EPS = 1e-6

OFF_Q, OFF_F, OFF_I, OFF_G = 0, HGRN_WIDTH, 2 * HGRN_WIDTH, 3 * HGRN_WIDTH
OFF_B = 4 * HGRN_WIDTH
OFF_C = OFF_B + CONV_WIDTH
OFF_H = OFF_C + CONV_WIDTH

PROMPT_TILE = 256
SAMPLE_SEQS = 16
V7X_VMEM_LIMIT = 58 * 1024 * 1024

ROW_LN1G, ROW_LN1B, ROW_LN2G, ROW_LN2B, ROW_LOGLB, ROW_GATE, ROW_CW01, ROW_CW2 = range(8)


def _dot(a, b):
    return jnp.dot(a, b, preferred_element_type=F32)


def _dot_nt(a, b):
    return lax.dot_general(a, b, (((1,), (1,)), ((), ())), preferred_element_type=F32)


def _dot_tn(a, b):
    return lax.dot_general(a, b, (((0,), (0,)), ((), ())), preferred_element_type=F32)


def _sigmoid(x):
    return 1.0 / (1.0 + jnp.exp(-x))


def _layer_norm(y, g, b):
    mu = jnp.mean(y, axis=-1, keepdims=True)
    yc = y - mu
    var = jnp.mean(yc * yc, axis=-1, keepdims=True)
    return yc * lax.rsqrt(var + LN_EPS) * g + b


def _front(x, win_ref, vec_ref, tri_ref, proj_scr, q_scr, kk_scr, lf_scr, b_scr):
    proj_scr[...] = _dot(x.astype(BF16), win_ref[...])
    qp = proj_scr[:, OFF_Q:OFF_Q + HGRN_WIDTH]
    q_scr[...] = qp * _sigmoid(qp)
    z = proj_scr[:, OFF_F:OFF_F + HGRN_WIDTH]
    log_lb = vec_ref[ROW_LOGLB:ROW_LOGLB + 1, 0:HGRN_WIDTH]
    log_1mlb = vec_ref[ROW_LOGLB:ROW_LOGLB + 1, HGRN_WIDTH:2 * HGRN_WIDTH]
    one_m_lb = vec_ref[ROW_GATE:ROW_GATE + 1, 0:HGRN_WIDTH]
    e = jnp.exp(-jnp.abs(z))
    log_sig = jnp.minimum(z, 0.0) - jnp.log1p(e)
    c = log_1mlb + log_sig
    lf = jnp.maximum(log_lb, c) + jnp.log1p(jnp.exp(-jnp.abs(log_lb - c)))
    lf_scr[...] = lf
    kk_scr[...] = one_m_lb * jnp.where(z >= 0, e, 1.0) / (1.0 + e)
    hi = lf.astype(BF16)
    r1 = lf - hi.astype(F32)
    mid = r1.astype(BF16)
    lo = (r1 - mid.astype(F32)).astype(BF16)
    tri = tri_ref[...]
    b_scr[...] = _dot(tri, hi) + _dot(tri, mid) + _dot(tri, lo)


def _intra_head(h, seg, tm, proj_scr, q_scr, kk_scr, lf_scr, b_scr, lv_ref):
    hs = slice(h * HEAD_DIM, (h + 1) * HEAD_DIM)
    qh = q_scr[:, hs]
    kh = kk_scr[:, hs]
    lfh = lf_scr[:, hs]
    bh = b_scr[:, hs]
    vh = proj_scr[:, OFF_I + h * HEAD_DIM:OFF_I + (h + 1) * HEAD_DIM]
    row = lax.broadcasted_iota(jnp.int32, (tm, HEAD_DIM), 0)
    lv = lv_ref[...]
    sc = jnp.zeros((tm, tm), F32)
    half = seg // 2
    while half >= 1:
        lg = int(math.log2(half))
        first = ((row >> lg) & 1) == 0
        if half >= 4:
            blk = 2 * half
            pieces = [jnp.broadcast_to(b_scr[pl.ds(blk * jb + half - 1, 1), hs], (blk, HEAD_DIM))
                      for jb in range(tm // blk)]
            bm = pieces[0] if len(pieces) == 1 else jnp.concatenate(pieces, axis=0)
            arg = jnp.where(first, bm - bh, bh - bm)
        elif half == 2:
            r = row & 3
            lf_next = pltpu.roll(lfh, tm - 1, 0)
            lf_prev = pltpu.roll(lfh, 1, 0)
            arg = jnp.where(r == 0, lf_next, jnp.where(r == 1, 0.0, jnp.where(r == 2, lfh, lfh + lf_prev)))
        else:
            arg = jnp.where(first, 0.0, lfh)
        xl = (jnp.where(first, kh, qh) * jnp.exp(arg)).astype(BF16)
        sc = jnp.where(lv == lg, _dot_nt(xl, xl), sc)
        half //= 2
    dg = jnp.sum(qh * kh, axis=-1, keepdims=True)
    return _dot(sc.astype(BF16), vh.astype(BF16)) + dg * vh


def _finish_head(h, o, proj_scr, vec_ref, o_scr):
    hs = slice(h * HEAD_DIM, (h + 1) * HEAD_DIM)
    g = proj_scr[:, OFF_G + h * HEAD_DIM:OFF_G + (h + 1) * HEAD_DIM]
    onorm = vec_ref[ROW_GATE:ROW_GATE + 1, HGRN_WIDTH + h * HEAD_DIM:HGRN_WIDTH + (h + 1) * HEAD_DIM]
    o = o * lax.rsqrt(jnp.mean(o * o, axis=-1, keepdims=True) + RMS_EPS)
    o_scr[:, hs] = o * onorm * (g * _sigmoid(g))


def _mix_norm(x, mixin, wout_ref, vec_ref):
    mix = _dot(mixin, wout_ref[...])
    return _layer_norm(ALPHA * x + mix, vec_ref[ROW_LN1G:ROW_LN1G + 1, :], vec_ref[ROW_LN1B:ROW_LN1B + 1, :])


def _ffn_chunk(c, x1b, w1_ref, w2_ref):
    cs = slice(c * FF_CHUNK, (c + 1) * FF_CHUNK)
    hid = jnp.maximum(_dot(x1b, w1_ref[:, cs]), 0.0)
    return _dot((hid * hid).astype(BF16), w2_ref[cs, :])


def _ffn_norm(x1, ffn, vec_ref):
    return _layer_norm(ALPHA * x1 + ffn, vec_ref[ROW_LN2G:ROW_LN2G + 1, :], vec_ref[ROW_LN2B:ROW_LN2B + 1, :])


def _back(x, mixin, wout_ref, w1_ref, w2_ref, vec_ref):
    x1 = _mix_norm(x, mixin, wout_ref, vec_ref)
    x1b = x1.astype(BF16)
    acc = _ffn_chunk(0, x1b, w1_ref, w2_ref)
    for c in range(1, N_FF_CHUNKS):
        acc = acc + _ffn_chunk(c, x1b, w1_ref, w2_ref)
    return _ffn_norm(x1, acc, vec_ref)


def _conv_weights(vec_ref):
    w0 = vec_ref[ROW_CW01:ROW_CW01 + 1, 0:CONV_WIDTH]
    w1 = vec_ref[ROW_CW01:ROW_CW01 + 1, CONV_WIDTH:2 * CONV_WIDTH]
    w2 = vec_ref[ROW_CW2:ROW_CW2 + 1, 0:CONV_WIDTH]
    return w0, w1, w2


def _prompt_kernel(x_ref, win_ref, wout_ref, w1_ref, w2_ref, vec_ref, lv_ref, tri_ref,
                   y_ref, hst_ref, cst_ref,
                   proj_scr, q_scr, kk_scr, lf_scr, b_scr, u_scr, o_scr, xs_scr, mix_scr,
                   x1_scr, x1b_scr, acc_scr, *, n_tiles, tiles_per_seq):
    assert N_FF_CHUNKS == N_HEADS
    tm = PROMPT_TILE
    i = pl.program_id(0)
    live = i < n_tiles
    j = jnp.minimum(i, n_tiles - 1) % tiles_per_seq

    @pl.when(i == 0)
    def _():
        xs_scr[...] = jnp.zeros(xs_scr.shape, F32)
        mix_scr[...] = jnp.zeros(mix_scr.shape, BF16)

    @pl.when(jnp.logical_and(j == 0, live))
    def _():
        hst_ref[...] = jnp.zeros(hst_ref.shape, F32)
        u_scr[0:8, :] = jnp.zeros((8, CONV_WIDTH), F32)

    x1 = _mix_norm(xs_scr[...], mix_scr[...], wout_ref, vec_ref)
    x1_scr[...] = x1
    x1b_scr[...] = x1.astype(BF16)

    x = x_ref[0]
    _front(x, win_ref, vec_ref, tri_ref, proj_scr, q_scr, kk_scr, lf_scr, b_scr)

    for h in range(N_HEADS):
        ffn = _ffn_chunk(h, x1b_scr[...], w1_ref, w2_ref)
        acc_scr[...] = ffn if h == 0 else acc_scr[...] + ffn
        hs = slice(h * HEAD_DIM, (h + 1) * HEAD_DIM)
        o = _intra_head(h, tm, tm, proj_scr, q_scr, kk_scr, lf_scr, b_scr, lv_ref)
        qh = q_scr[:, hs]
        kh = kk_scr[:, hs]
        bh = b_scr[:, hs]
        vh = proj_scr[:, OFF_I + h * HEAD_DIM:OFF_I + (h + 1) * HEAD_DIM]
        s_prev = hst_ref[0, h]
        o = o + _dot((qh * jnp.exp(bh)).astype(BF16), s_prev.astype(BF16))
        b_last = b_scr[tm - 1:tm, hs]
        kd = (kh * jnp.exp(b_last - bh)).astype(BF16)
        upd = _dot_tn(kd, vh.astype(BF16))
        a_col = jnp.exp(jnp.transpose(b_scr[tm - 8:tm, hs])[:, 7:8])
        hst_ref[0, h] = jnp.where(live, a_col * s_prev + upd, s_prev)
        _finish_head(h, o, proj_scr, vec_ref, o_scr)

    u = proj_scr[:, OFF_C:OFF_C + CONV_WIDTH] * proj_scr[:, OFF_H:OFF_H + CONV_WIDTH]
    u_scr[8:8 + tm, :] = u
    w0, w1, w2 = _conv_weights(vec_ref)
    conv = w0 * u_scr[6:6 + tm, :] + w1 * u_scr[7:7 + tm, :] + w2 * u
    yc = proj_scr[:, OFF_B:OFF_B + CONV_WIDTH] * conv
    tail = u_scr[tm:tm + 8, :]
    u_scr[0:8, :] = tail
    cst_ref[0] = tail[6:8, :]

    y_ref[0] = _ffn_norm(x1_scr[...], acc_scr[...], vec_ref)
    xs_scr[...] = x
    mix_scr[...] = jnp.concatenate([o_scr[...], yc], axis=-1).astype(BF16)


def _sample_kernel(x_ref, sin_ref, ext_ref, win_ref, wout_ref, w1_ref, w2_ref, vec_ref, lv_ref, tri_ref,
                   y_ref, sout_ref, u_ref,
                   proj_scr, q_scr, kk_scr, lf_scr, b_scr, o_scr, qe_scr, kd_scr, vb_scr):
    seg = 8
    tm = SAMPLE_SEQS * seg
    x = x_ref[...]
    _front(x, win_ref, vec_ref, tri_ref, proj_scr, q_scr, kk_scr, lf_scr, b_scr)

    for h in range(N_HEADS):
        hs = slice(h * HEAD_DIM, (h + 1) * HEAD_DIM)
        o_scr[:, hs] = _intra_head(h, seg, tm, proj_scr, q_scr, kk_scr, lf_scr, b_scr, lv_ref)
        bh = b_scr[:, hs]
        pieces = [jnp.broadcast_to(b_scr[pl.ds(seg * jb + seg - 1, 1), hs], (seg, HEAD_DIM))
                  for jb in range(tm // seg)]
        b_last = jnp.concatenate(pieces, axis=0)
        qe_scr[:, hs] = (q_scr[:, hs] * jnp.exp(bh)).astype(BF16)
        kd_scr[:, hs] = (kk_scr[:, hs] * jnp.exp(b_last - bh)).astype(BF16)
    vb_scr[...] = proj_scr[:, OFF_I:OFF_I + HGRN_WIDTH].astype(BF16)

    top = lax.broadcasted_iota(jnp.int32, (2 * seg, HEAD_DIM), 0) < seg

    def pair_body(p, carry):
        r0 = pl.multiple_of(p * 2 * seg, 2 * seg)
        for h in range(N_HEADS):
            hs = slice(h * HEAD_DIM, (h + 1) * HEAD_DIM)
            qe = qe_scr[pl.ds(r0, 2 * seg), hs]
            kd = kd_scr[pl.ds(r0, 2 * seg), hs]
            vv = vb_scr[pl.ds(r0, 2 * seg), hs]
            bt = jnp.transpose(b_scr[pl.ds(r0, 2 * seg), hs])
            s_a = sin_ref[2 * p, h]
            s_b = sin_ref[2 * p + 1, h]
            o_a = _dot(qe, s_a.astype(BF16))
            o_b = _dot(qe, s_b.astype(BF16))
            o_scr[pl.ds(r0, 2 * seg), hs] = o_scr[pl.ds(r0, 2 * seg), hs] + jnp.where(top, o_a, o_b)
            zero = jnp.zeros_like(kd)
            sout_ref[2 * p, h] = jnp.exp(bt[:, seg - 1:seg]) * s_a + _dot_tn(jnp.where(top, kd, zero), vv)
            sout_ref[2 * p + 1, h] = jnp.exp(bt[:, 2 * seg - 1:2 * seg]) * s_b + _dot_tn(jnp.where(top, zero, kd), vv)
        return carry

    lax.fori_loop(0, SAMPLE_SEQS // 2, pair_body, 0)

    for h in range(N_HEADS):
        hs = slice(h * HEAD_DIM, (h + 1) * HEAD_DIM)
        _finish_head(h, o_scr[:, hs], proj_scr, vec_ref, o_scr)

    u = proj_scr[:, OFF_C:OFF_C + CONV_WIDTH] * proj_scr[:, OFF_H:OFF_H + CONV_WIDTH]
    u_ref[...] = u
    ext = ext_ref[...]
    tmod = lax.broadcasted_iota(jnp.int32, (tm, CONV_WIDTH), 0) & (seg - 1)
    u1 = jnp.where(tmod == 0, pltpu.roll(ext, tm - 1, 0), pltpu.roll(u, 1, 0))
    u2 = jnp.where(tmod < 2, ext, pltpu.roll(u, 2, 0))
    w0, w1, w2 = _conv_weights(vec_ref)
    yc = proj_scr[:, OFF_B:OFF_B + CONV_WIDTH] * (w0 * u2 + w1 * u1 + w2 * u)

    mixin = jnp.concatenate([o_scr[...], yc], axis=-1).astype(BF16)
    y_ref[...] = _back(x, mixin, wout_ref, w1_ref, w2_ref, vec_ref)


def _level_table(tm, seg):
    t = np.arange(tm)[:, None]
    s = np.arange(tm)[None, :]
    x = np.maximum(t ^ s, 1)
    lv = np.floor(np.log2(x)).astype(np.int32)
    ok = (s < t) & ((t // seg) == (s // seg))
    return np.where(ok, lv, -1).astype(np.int32)


def _tri_table(tm, seg):
    t = np.arange(tm)[:, None]
    s = np.arange(tm)[None, :]
    return ((s <= t) & ((t // seg) == (s // seg))).astype(np.float32)


def _const_spec(shape):
    nd = len(shape)
    return pl.BlockSpec(shape, lambda *_: (0,) * nd, pipeline_mode=pl.Buffered(1))


def _weight_specs(layer):
    def wspec(k, n):
        return pl.BlockSpec((None, k, n), lambda *_: (layer, 0, 0), pipeline_mode=pl.Buffered(1))
    return [wspec(D_MODEL, PROJ_WIDTH), wspec(D_MODEL, D_MODEL), wspec(D_MODEL, D_FF), wspec(D_FF, D_MODEL),
            pl.BlockSpec((None, 8, D_MODEL), lambda *_: (layer, 0, 0), pipeline_mode=pl.Buffered(1))]


def _prompt_layer(layer, x, weights, vecs):
    bsz, seq, _ = x.shape
    tm = PROMPT_TILE
    tiles_per_seq = seq // tm
    n_tiles = bsz * tiles_per_seq
    lv = jnp.asarray(_level_table(tm, tm))
    tri = jnp.asarray(_tri_table(tm, tm), dtype=BF16)

    def cur(i):
        t = jnp.minimum(i, n_tiles - 1)
        return t // tiles_per_seq, t % tiles_per_seq

    def prev(i):
        t = jnp.maximum(i - 1, 0)
        return t // tiles_per_seq, t % tiles_per_seq

    return pl.pallas_call(
        functools.partial(_prompt_kernel, n_tiles=n_tiles, tiles_per_seq=tiles_per_seq),
        grid=(n_tiles + 1,),
        in_specs=[pl.BlockSpec((1, tm, D_MODEL), lambda i: (*cur(i), 0))] + _weight_specs(layer)
        + [_const_spec((tm, tm)), _const_spec((tm, tm))],
        out_specs=[pl.BlockSpec((1, tm, D_MODEL), lambda i: (*prev(i), 0)),
                   pl.BlockSpec((1, N_HEADS, HEAD_DIM, HEAD_DIM), lambda i: (cur(i)[0], 0, 0, 0)),
                   pl.BlockSpec((1, CONV_K - 1, CONV_WIDTH), lambda i: (cur(i)[0], 0, 0))],
        out_shape=[jax.ShapeDtypeStruct((bsz, seq, D_MODEL), F32),
                   jax.ShapeDtypeStruct((bsz, N_HEADS, HEAD_DIM, HEAD_DIM), F32),
                   jax.ShapeDtypeStruct((bsz, CONV_K - 1, CONV_WIDTH), F32)],
        scratch_shapes=[pltpu.VMEM((tm, PROJ_WIDTH), F32),
                        pltpu.VMEM((tm, HGRN_WIDTH), F32), pltpu.VMEM((tm, HGRN_WIDTH), F32),
                        pltpu.VMEM((tm, HGRN_WIDTH), F32), pltpu.VMEM((tm, HGRN_WIDTH), F32),
                        pltpu.VMEM((tm + 8, CONV_WIDTH), F32), pltpu.VMEM((tm, HGRN_WIDTH), F32),
                        pltpu.VMEM((tm, D_MODEL), F32), pltpu.VMEM((tm, D_MODEL), BF16),
                        pltpu.VMEM((tm, D_MODEL), F32), pltpu.VMEM((tm, D_MODEL), BF16),
                        pltpu.VMEM((tm, D_MODEL), F32)],
        compiler_params=pltpu.CompilerParams(dimension_semantics=("arbitrary",),
                                             vmem_limit_bytes=V7X_VMEM_LIMIT),
        name=f"prompt_layer{layer}",
    )(x, *weights, vecs, lv, tri)


def _sample_layer(layer, x, state, ext, weights, vecs):
    rows = x.shape[0]
    seg = 8
    tm = SAMPLE_SEQS * seg
    n_seq = rows // seg
    lv = jnp.asarray(_level_table(tm, seg))
    tri = jnp.asarray(_tri_table(tm, seg), dtype=BF16)
    st_spec = pl.BlockSpec((None, SAMPLE_SEQS, N_HEADS, HEAD_DIM, HEAD_DIM), lambda i: (layer, i, 0, 0, 0))
    return pl.pallas_call(
        _sample_kernel,
        grid=(n_seq // SAMPLE_SEQS,),
        in_specs=[pl.BlockSpec((tm, D_MODEL), lambda i: (i, 0)), st_spec,
                  pl.BlockSpec((None, tm, CONV_WIDTH), lambda i: (layer, i, 0))] + _weight_specs(layer)
        + [_const_spec((tm, tm)), _const_spec((tm, tm))],
        out_specs=[pl.BlockSpec((tm, D_MODEL), lambda i: (i, 0)),
                   pl.BlockSpec((SAMPLE_SEQS, N_HEADS, HEAD_DIM, HEAD_DIM), lambda i: (i, 0, 0, 0)),
                   pl.BlockSpec((tm, CONV_WIDTH), lambda i: (i, 0))],
        out_shape=[jax.ShapeDtypeStruct((rows, D_MODEL), F32),
                   jax.ShapeDtypeStruct((n_seq, N_HEADS, HEAD_DIM, HEAD_DIM), F32),
                   jax.ShapeDtypeStruct((rows, CONV_WIDTH), F32)],
        scratch_shapes=[pltpu.VMEM((tm, PROJ_WIDTH), F32),
                        pltpu.VMEM((tm, HGRN_WIDTH), F32), pltpu.VMEM((tm, HGRN_WIDTH), F32),
                        pltpu.VMEM((tm, HGRN_WIDTH), F32), pltpu.VMEM((tm, HGRN_WIDTH), F32),
                        pltpu.VMEM((tm, HGRN_WIDTH), F32),
                        pltpu.VMEM((tm, HGRN_WIDTH), BF16), pltpu.VMEM((tm, HGRN_WIDTH), BF16),
                        pltpu.VMEM((tm, HGRN_WIDTH), BF16)],
        compiler_params=pltpu.CompilerParams(dimension_semantics=("arbitrary",),
                                             vmem_limit_bytes=V7X_VMEM_LIMIT),
        name=f"sample_layer{layer}",
    )(x, state, ext, *weights, vecs, lv, tri)


def kernel(x_prompt, x_sample, state_hgrn, state_conv, w_in, lb_logits, conv_w, onorm_g, w_out,
           ln1_g, ln1_b, w_ff1, w_ff2, ln2_g, ln2_b):
    n_seq, dec_seq, _ = x_sample.shape
    p = jax.nn.softmax(lb_logits.astype(F32), axis=0)
    cum = jnp.cumsum(p, axis=0)
    lb = cum - cum[0:1]
    zeros_half = jnp.zeros((DEPTH, CONV_WIDTH), F32)
    vecs = jnp.stack([
        ln1_g, ln1_b, ln2_g, ln2_b,
        jnp.concatenate([jnp.log(lb), jnp.log1p(-lb)], axis=-1),
        jnp.concatenate([1.0 - lb, onorm_g], axis=-1),
        jnp.concatenate([conv_w[:, 0], conv_w[:, 1]], axis=-1),
        jnp.concatenate([conv_w[:, 2], zeros_half], axis=-1)], axis=1).astype(F32)
    weights = (w_in.astype(BF16), w_out.astype(BF16), w_ff1.astype(BF16), w_ff2.astype(BF16))
    ext = jnp.pad(state_conv, ((0, 0), (0, 0), (0, dec_seq - (CONV_K - 1)), (0, 0)))
    ext = ext.reshape(DEPTH, n_seq * dec_seq, CONV_WIDTH)

    yp = x_prompt
    ys = x_sample.reshape(n_seq * dec_seq, D_MODEL)
    hp, cp, hs_, cs = [], [], [], []
    for layer in range(DEPTH):
        yp, hgrn_p, conv_p = _prompt_layer(layer, yp, weights, vecs)
        ys, hgrn_s, u_s = _sample_layer(layer, ys, state_hgrn, ext, weights, vecs)
        hp.append(hgrn_p)
        cp.append(conv_p)
        hs_.append(hgrn_s)
        cs.append(u_s.reshape(n_seq, dec_seq, CONV_WIDTH)[:, dec_seq - (CONV_K - 1):])
    return (yp, ys.reshape(n_seq, dec_seq, D_MODEL), jnp.stack(hp), jnp.stack(cp), jnp.stack(hs_), jnp.stack(cs))
```

```python
import functools
import math

import numpy as np
import jax
import jax.numpy as jnp
from jax import lax
from jax.experimental import pallas as pl
from jax.experimental.pallas import tpu as pltpu

F32 = jnp.float32
BF16 = jnp.bfloat16

D_MODEL = 1024
DEPTH = 2
HGRN_WIDTH = 512
CONV_WIDTH = 512
HEAD_DIM = 128
N_HEADS = HGRN_WIDTH // HEAD_DIM
CONV_K = 3
D_FF = 4 * D_MODEL
PROJ_WIDTH = 4 * HGRN_WIDTH + 3 * CONV_WIDTH
FF_CHUNK = 1024
N_FF_CHUNKS = D_FF // FF_CHUNK
ALPHA = float((2 * DEPTH) ** 0.25)
LN_EPS = 1e-5
RMS_EPS = 1e-6

OFF_Q, OFF_F, OFF_I, OFF_G = 0, HGRN_WIDTH, 2 * HGRN_WIDTH, 3 * HGRN_WIDTH
OFF_B = 4 * HGRN_WIDTH
OFF_C = OFF_B + CONV_WIDTH
OFF_H = OFF_C + CONV_WIDTH

LOG2_E = math.log2(math.e)
DIAG = 128
PROMPT_TILE = 256
SAMPLE_SEQS = 16
V7X_VMEM_LIMIT = 58 * 1024 * 1024

ROW_LN1G, ROW_LN1B, ROW_LN2G, ROW_LN2B, ROW_LOGLB, ROW_GATE, ROW_CW01, ROW_CW2 = range(8)


def _dot(a, b):
    return jnp.dot(a, b, preferred_element_type=F32)


def _dot_nt(a, b):
    return lax.dot_general(a, b, (((1,), (1,)), ((), ())), preferred_element_type=F32)


def _dot_tn(a, b):
    return lax.dot_general(a, b, (((0,), (0,)), ((), ())), preferred_element_type=F32)


def _sigmoid(x):
    return 1.0 / (1.0 + jnp.exp(-x))


def _layer_norm(y, g, b):
    mu = jnp.mean(y, axis=-1, keepdims=True)
    yc = y - mu
    var = jnp.mean(yc * yc, axis=-1, keepdims=True)
    return yc * lax.rsqrt(var + LN_EPS) * g + b


def _front(x, win_ref, vec_ref, tri_ref, proj_scr, q_scr, kk_scr, lf_scr, b_scr):
    proj_scr[...] = _dot(x.astype(BF16), win_ref[...])
    qp = proj_scr[:, OFF_Q:OFF_Q + HGRN_WIDTH]
    q_scr[...] = qp * _sigmoid(qp)
    z = proj_scr[:, OFF_F:OFF_F + HGRN_WIDTH]
    log_lb = vec_ref[ROW_LOGLB:ROW_LOGLB + 1, 0:HGRN_WIDTH]
    log_1mlb = vec_ref[ROW_LOGLB:ROW_LOGLB + 1, HGRN_WIDTH:2 * HGRN_WIDTH]
    one_m_lb = vec_ref[ROW_GATE:ROW_GATE + 1, 0:HGRN_WIDTH]
    e = jnp.exp(-jnp.abs(z))
    log_sig = jnp.minimum(z, 0.0) - jnp.log1p(e)
    c = log_1mlb + log_sig
    lf = LOG2_E * (jnp.maximum(log_lb, c) + jnp.log1p(jnp.exp(-jnp.abs(log_lb - c))))
    lf_scr[...] = lf
    kk_scr[...] = one_m_lb * jnp.where(z >= 0, e, 1.0) / (1.0 + e)
    hi = lf.astype(BF16)
    r1 = lf - hi.astype(F32)
    mid = r1.astype(BF16)
    lo = (r1 - mid.astype(F32)).astype(BF16)
    tri = tri_ref[...]
    b_scr[...] = _dot(tri, hi) + _dot(tri, mid) + _dot(tri, lo)


def _level_operand(h, half, tm, q_scr, kk_scr, lf_scr, b_scr):
    hs = slice(h * HEAD_DIM, (h + 1) * HEAD_DIM)
    if half >= 8:
        blk = 2 * half
        pieces = []
        for r0 in range(0, tm, blk):
            bm = b_scr[r0 + half - 1:r0 + half, hs]
            pieces.append(kk_scr[r0:r0 + half, hs] * jnp.exp2(bm - b_scr[r0:r0 + half, hs]))
            pieces.append(q_scr[r0 + half:r0 + blk, hs] * jnp.exp2(b_scr[r0 + half:r0 + blk, hs] - bm))
        return jnp.concatenate(pieces, axis=0).astype(BF16)
    row = lax.broadcasted_iota(jnp.int32, (tm, HEAD_DIM), 0)
    first = (row & half) == 0
    lfh = lf_scr[:, hs]
    if half == 4:
        bh = b_scr[:, hs]
        bm = jnp.concatenate([jnp.broadcast_to(b_scr[r0 + 3:r0 + 4, hs], (8, HEAD_DIM))
                              for r0 in range(0, tm, 8)], axis=0)
        arg = jnp.where(first, bm - bh, bh - bm)
    elif half == 2:
        r = row & 3
        lf_next = pltpu.roll(lfh, tm - 1, 0)
        lf_prev = pltpu.roll(lfh, 1, 0)
        arg = jnp.where(r == 0, lf_next, jnp.where(r == 1, 0.0, jnp.where(r == 2, lfh, lfh + lf_prev)))
    else:
        arg = jnp.where(first, 0.0, lfh)
    return (jnp.where(first, kk_scr[:, hs], q_scr[:, hs]) * jnp.exp2(arg)).astype(BF16)


def _intra_head(h, seg, tm, proj_scr, q_scr, kk_scr, lf_scr, b_scr, lv_ref):
    hs = slice(h * HEAD_DIM, (h + 1) * HEAD_DIM)
    vh = proj_scr[:, OFF_I + h * HEAD_DIM:OFF_I + (h + 1) * HEAD_DIM]
    n_diag = tm // DIAG
    lv = lv_ref[...]
    diag = [None] * n_diag
    cross = {}
    half = seg // 2
    while half >= 1:
        xl = _level_operand(h, half, tm, q_scr, kk_scr, lf_scr, b_scr)
        if half >= DIAG:
            for r0 in range(0, tm, 2 * half):
                for tq in range(r0 + half, r0 + 2 * half, DIAG):
                    for tk in range(r0, r0 + half, DIAG):
                        cross[(tq // DIAG, tk // DIAG)] = _dot_nt(xl[tq:tq + DIAG], xl[tk:tk + DIAG])
        else:
            lg = int(math.log2(half))
            for d in range(n_diag):
                xd = xl[d * DIAG:(d + 1) * DIAG]
                p = _dot_nt(xd, xd)
                diag[d] = jnp.where(lv == lg, p, 0.0 if diag[d] is None else diag[d])
        half //= 2
    vb = vh.astype(BF16)
    outs = []
    for d in range(n_diag):
        p = jnp.concatenate([cross[(d, e)] for e in range(d)] + [diag[d]], axis=1).astype(BF16)
        outs.append(_dot(p, vb[0:(d + 1) * DIAG]))
    dg = jnp.sum(q_scr[:, hs] * kk_scr[:, hs], axis=-1, keepdims=True)
    return jnp.concatenate(outs, axis=0) + dg * vh


def _finish_head(h, o, proj_scr, vec_ref, o_scr):
    hs = slice(h * HEAD_DIM, (h + 1) * HEAD_DIM)
    g = proj_scr[:, OFF_G + h * HEAD_DIM:OFF_G + (h + 1) * HEAD_DIM]
    onorm = vec_ref[ROW_GATE:ROW_GATE + 1, HGRN_WIDTH + h * HEAD_DIM:HGRN_WIDTH + (h + 1) * HEAD_DIM]
    o = o * lax.rsqrt(jnp.mean(o * o, axis=-1, keepdims=True) + RMS_EPS)
    o_scr[:, hs] = o * onorm * (g * _sigmoid(g))


def _mix_norm(x, mixin, wout_ref, vec_ref):
    mix = _dot(mixin, wout_ref[...])
    return _layer_norm(ALPHA * x + mix, vec_ref[ROW_LN1G:ROW_LN1G + 1, :], vec_ref[ROW_LN1B:ROW_LN1B + 1, :])


def _ffn_chunk(c, x1b, w1_ref, w2_ref):
    cs = slice(c * FF_CHUNK, (c + 1) * FF_CHUNK)
    hid = jnp.maximum(_dot(x1b, w1_ref[:, cs]), 0.0)
    return _dot((hid * hid).astype(BF16), w2_ref[cs, :])


def _ffn_norm(x1, ffn, vec_ref):
    return _layer_norm(ALPHA * x1 + ffn, vec_ref[ROW_LN2G:ROW_LN2G + 1, :], vec_ref[ROW_LN2B:ROW_LN2B + 1, :])


def _back(x, mixin, wout_ref, w1_ref, w2_ref, vec_ref):
    x1 = _mix_norm(x, mixin, wout_ref, vec_ref)
    x1b = x1.astype(BF16)
    acc = _ffn_chunk(0, x1b, w1_ref, w2_ref)
    for c in range(1, N_FF_CHUNKS):
        acc = acc + _ffn_chunk(c, x1b, w1_ref, w2_ref)
    return _ffn_norm(x1, acc, vec_ref)


def _conv_weights(vec_ref):
    w0 = vec_ref[ROW_CW01:ROW_CW01 + 1, 0:CONV_WIDTH]
    w1 = vec_ref[ROW_CW01:ROW_CW01 + 1, CONV_WIDTH:2 * CONV_WIDTH]
    w2 = vec_ref[ROW_CW2:ROW_CW2 + 1, 0:CONV_WIDTH]
    return w0, w1, w2


def _prompt_kernel(x_ref, win_ref, wout_ref, w1_ref, w2_ref, vec_ref, lv_ref, tri_ref,
                   y_ref, hst_ref, cst_ref,
                   proj_scr, q_scr, kk_scr, lf_scr, b_scr, u_scr, o_scr, xs_scr, mix_scr,
                   x1_scr, x1b_scr, acc_scr, *, n_tiles, tiles_per_seq):
    assert N_FF_CHUNKS == N_HEADS
    tm = PROMPT_TILE
    i = pl.program_id(0)
    live = i < n_tiles
    j = jnp.minimum(i, n_tiles - 1) % tiles_per_seq

    @pl.when(i == 0)
    def _():
        xs_scr[...] = jnp.zeros(xs_scr.shape, F32)
        mix_scr[...] = jnp.zeros(mix_scr.shape, BF16)

    @pl.when(jnp.logical_and(j == 0, live))
    def _():
        hst_ref[...] = jnp.zeros(hst_ref.shape, F32)
        u_scr[0:8, :] = jnp.zeros((8, CONV_WIDTH), F32)

    x1 = _mix_norm(xs_scr[...], mix_scr[...], wout_ref, vec_ref)
    x1_scr[...] = x1
    x1b_scr[...] = x1.astype(BF16)

    x = x_ref[0]
    _front(x, win_ref, vec_ref, tri_ref, proj_scr, q_scr, kk_scr, lf_scr, b_scr)

    for h in range(N_HEADS):
        ffn = _ffn_chunk(h, x1b_scr[...], w1_ref, w2_ref)
        acc_scr[...] = ffn if h == 0 else acc_scr[...] + ffn
        hs = slice(h * HEAD_DIM, (h + 1) * HEAD_DIM)
        o = _intra_head(h, tm, tm, proj_scr, q_scr, kk_scr, lf_scr, b_scr, lv_ref)
        qh = q_scr[:, hs]
        kh = kk_scr[:, hs]
        bh = b_scr[:, hs]
        vh = proj_scr[:, OFF_I + h * HEAD_DIM:OFF_I + (h + 1) * HEAD_DIM]
        s_prev = hst_ref[0, h]
        o = o + _dot((qh * jnp.exp2(bh)).astype(BF16), s_prev.astype(BF16))
        b_last = b_scr[tm - 1:tm, hs]
        kd = (kh * jnp.exp2(b_last - bh)).astype(BF16)
        upd = _dot_tn(kd, vh.astype(BF16))
        a_col = jnp.exp2(jnp.transpose(b_scr[tm - 8:tm, hs])[:, 7:8])
        hst_ref[0, h] = jnp.where(live, a_col * s_prev + upd, s_prev)
        _finish_head(h, o, proj_scr, vec_ref, o_scr)

    u = proj_scr[:, OFF_C:OFF_C + CONV_WIDTH] * proj_scr[:, OFF_H:OFF_H + CONV_WIDTH]
    u_scr[8:8 + tm, :] = u
    w0, w1, w2 = _conv_weights(vec_ref)
    conv = w0 * u_scr[6:6 + tm, :] + w1 * u_scr[7:7 + tm, :] + w2 * u
    yc = proj_scr[:, OFF_B:OFF_B + CONV_WIDTH] * conv
    tail = u_scr[tm:tm + 8, :]
    u_scr[0:8, :] = tail
    cst_ref[0] = tail[6:8, :]

    y_ref[0] = _ffn_norm(x1_scr[...], acc_scr[...], vec_ref)
    xs_scr[...] = x
    mix_scr[...] = jnp.concatenate([o_scr[...], yc], axis=-1).astype(BF16)


def _sample_kernel(x_ref, sin_ref, ext_ref, win_ref, wout_ref, w1_ref, w2_ref, vec_ref, lv_ref, tri_ref,
                   y_ref, sout_ref, u_ref,
                   proj_scr, q_scr, kk_scr, lf_scr, b_scr, o_scr, qe_scr, kd_scr, vb_scr, ys_scr):
    seg = 8
    tm = SAMPLE_SEQS * seg
    rows = pl.ds(pl.multiple_of(pl.program_id(1) * tm, tm), tm)

    @pl.when(pl.program_id(0) == 0)
    def _():
        ys_scr[rows, :] = x_ref[...]

    x = ys_scr[rows, :]
    _front(x, win_ref, vec_ref, tri_ref, proj_scr, q_scr, kk_scr, lf_scr, b_scr)

    for h in range(N_HEADS):
        hs = slice(h * HEAD_DIM, (h + 1) * HEAD_DIM)
        o_scr[:, hs] = _intra_head(h, seg, tm, proj_scr, q_scr, kk_scr, lf_scr, b_scr, lv_ref)
        bh = b_scr[:, hs]
        pieces = [jnp.broadcast_to(b_scr[pl.ds(seg * jb + seg - 1, 1), hs], (seg, HEAD_DIM))
                  for jb in range(tm // seg)]
        b_last = jnp.concatenate(pieces, axis=0)
        qe_scr[:, hs] = (q_scr[:, hs] * jnp.exp2(bh)).astype(BF16)
        kd_scr[:, hs] = (kk_scr[:, hs] * jnp.exp2(b_last - bh)).astype(BF16)
    vb_scr[...] = proj_scr[:, OFF_I:OFF_I + HGRN_WIDTH].astype(BF16)

    top = lax.broadcasted_iota(jnp.int32, (2 * seg, HEAD_DIM), 0) < seg

    def pair_body(p, carry):
        r0 = pl.multiple_of(p * 2 * seg, 2 * seg)
        for h in range(N_HEADS):
            hs = slice(h * HEAD_DIM, (h + 1) * HEAD_DIM)
            qe = qe_scr[pl.ds(r0, 2 * seg), hs]
            kd = kd_scr[pl.ds(r0, 2 * seg), hs]
            vv = vb_scr[pl.ds(r0, 2 * seg), hs]
            bt = jnp.transpose(b_scr[pl.ds(r0, 2 * seg), hs])
            s_a = sin_ref[2 * p, h]
            s_b = sin_ref[2 * p + 1, h]
            o_a = _dot(qe, s_a.astype(BF16))
            o_b = _dot(qe, s_b.astype(BF16))
            o_scr[pl.ds(r0, 2 * seg), hs] = o_scr[pl.ds(r0, 2 * seg), hs] + jnp.where(top, o_a, o_b)
            zero = jnp.zeros_like(kd)
            sout_ref[2 * p, h] = jnp.exp2(bt[:, seg - 1:seg]) * s_a + _dot_tn(jnp.where(top, kd, zero), vv)
            sout_ref[2 * p + 1, h] = jnp.exp2(bt[:, 2 * seg - 1:2 * seg]) * s_b + _dot_tn(jnp.where(top, zero, kd), vv)
        return carry

    lax.fori_loop(0, SAMPLE_SEQS // 2, pair_body, 0)

    for h in range(N_HEADS):
        hs = slice(h * HEAD_DIM, (h + 1) * HEAD_DIM)
        _finish_head(h, o_scr[:, hs], proj_scr, vec_ref, o_scr)

    u = proj_scr[:, OFF_C:OFF_C + CONV_WIDTH] * proj_scr[:, OFF_H:OFF_H + CONV_WIDTH]
    u_ref[...] = u
    ext = ext_ref[...]
    tmod = lax.broadcasted_iota(jnp.int32, (tm, CONV_WIDTH), 0) & (seg - 1)
    u1 = jnp.where(tmod == 0, pltpu.roll(ext, tm - 1, 0), pltpu.roll(u, 1, 0))
    u2 = jnp.where(tmod < 2, ext, pltpu.roll(u, 2, 0))
    w0, w1, w2 = _conv_weights(vec_ref)
    yc = proj_scr[:, OFF_B:OFF_B + CONV_WIDTH] * (w0 * u2 + w1 * u1 + w2 * u)

    mixin = jnp.concatenate([o_scr[...], yc], axis=-1).astype(BF16)
    y = _back(x, mixin, wout_ref, w1_ref, w2_ref, vec_ref)
    ys_scr[rows, :] = y
    y_ref[...] = y


def _level_table(tm, seg):
    t = np.arange(tm)[:, None]
    s = np.arange(tm)[None, :]
    x = np.maximum(t ^ s, 1)
    lv = np.floor(np.log2(x)).astype(np.int32)
    ok = (s < t) & ((t // seg) == (s // seg))
    return np.where(ok, lv, -1).astype(np.int32)


def _tri_table(tm, seg):
    t = np.arange(tm)[:, None]
    s = np.arange(tm)[None, :]
    return ((s <= t) & ((t // seg) == (s // seg))).astype(np.float32)


def _const_spec(shape):
    nd = len(shape)
    return pl.BlockSpec(shape, lambda *_: (0,) * nd, pipeline_mode=pl.Buffered(1))


def _weight_specs(layer_of):
    def wspec(k, n):
        return pl.BlockSpec((None, k, n), lambda *g: (layer_of(*g), 0, 0), pipeline_mode=pl.Buffered(1))
    return [wspec(D_MODEL, PROJ_WIDTH), wspec(D_MODEL, D_MODEL), wspec(D_MODEL, D_FF), wspec(D_FF, D_MODEL),
            wspec(8, D_MODEL)]


def _prompt_layer(layer, x, weights, vecs):
    bsz, seq, _ = x.shape
    tm = PROMPT_TILE
    tiles_per_seq = seq // tm
    n_tiles = bsz * tiles_per_seq
    lv = jnp.asarray(_level_table(DIAG, DIAG))
    tri = jnp.asarray(_tri_table(tm, tm), dtype=BF16)

    def cur(i):
        t = jnp.minimum(i, n_tiles - 1)
        return t // tiles_per_seq, t % tiles_per_seq

    def prev(i):
        t = jnp.maximum(i - 1, 0)
        return t // tiles_per_seq, t % tiles_per_seq

    return pl.pallas_call(
        functools.partial(_prompt_kernel, n_tiles=n_tiles, tiles_per_seq=tiles_per_seq),
        grid=(n_tiles + 1,),
        in_specs=[pl.BlockSpec((1, tm, D_MODEL), lambda i: (*cur(i), 0))] + _weight_specs(lambda i: layer)
        + [_const_spec((DIAG, DIAG)), _const_spec((tm, tm))],
        out_specs=[pl.BlockSpec((1, tm, D_MODEL), lambda i: (*prev(i), 0)),
                   pl.BlockSpec((1, N_HEADS, HEAD_DIM, HEAD_DIM), lambda i: (cur(i)[0], 0, 0, 0)),
                   pl.BlockSpec((1, CONV_K - 1, CONV_WIDTH), lambda i: (cur(i)[0], 0, 0))],
        out_shape=[jax.ShapeDtypeStruct((bsz, seq, D_MODEL), F32),
                   jax.ShapeDtypeStruct((bsz, N_HEADS, HEAD_DIM, HEAD_DIM), F32),
                   jax.ShapeDtypeStruct((bsz, CONV_K - 1, CONV_WIDTH), F32)],
        scratch_shapes=[pltpu.VMEM((tm, PROJ_WIDTH), F32),
                        pltpu.VMEM((tm, HGRN_WIDTH), F32), pltpu.VMEM((tm, HGRN_WIDTH), F32),
                        pltpu.VMEM((tm, HGRN_WIDTH), F32), pltpu.VMEM((tm, HGRN_WIDTH), F32),
                        pltpu.VMEM((tm + 8, CONV_WIDTH), F32), pltpu.VMEM((tm, HGRN_WIDTH), F32),
                        pltpu.VMEM((tm, D_MODEL), F32), pltpu.VMEM((tm, D_MODEL), BF16),
                        pltpu.VMEM((tm, D_MODEL), F32), pltpu.VMEM((tm, D_MODEL), BF16),
                        pltpu.VMEM((tm, D_MODEL), F32)],
        compiler_params=pltpu.CompilerParams(dimension_semantics=("arbitrary",),
                                             vmem_limit_bytes=V7X_VMEM_LIMIT),
        name=f"prompt_layer{layer}",
    )(x, *weights, vecs, lv, tri)


def _sample_layers(x, state, ext, weights, vecs):
    rows = x.shape[0]
    seg = 8
    tm = SAMPLE_SEQS * seg
    n_seq = rows // seg
    n_steps = n_seq // SAMPLE_SEQS
    lv = jnp.asarray(_level_table(DIAG, seg))
    tri = jnp.asarray(_tri_table(tm, seg), dtype=BF16)
    st_spec = pl.BlockSpec((None, SAMPLE_SEQS, N_HEADS, HEAD_DIM, HEAD_DIM), lambda l, i: (l, i, 0, 0, 0))
    u_spec = pl.BlockSpec((None, tm, CONV_WIDTH), lambda l, i: (l, i, 0))
    return pl.pallas_call(
        _sample_kernel,
        grid=(DEPTH, n_steps),
        in_specs=[pl.BlockSpec((tm, D_MODEL), lambda l, i: (i * (1 - l) + (n_steps - 1) * l, 0)), st_spec, u_spec]
        + _weight_specs(lambda l, i: l) + [_const_spec((DIAG, DIAG)), _const_spec((tm, tm))],
        out_specs=[pl.BlockSpec((tm, D_MODEL), lambda l, i: (i, 0)), st_spec, u_spec],
        out_shape=[jax.ShapeDtypeStruct((rows, D_MODEL), F32),
                   jax.ShapeDtypeStruct((DEPTH, n_seq, N_HEADS, HEAD_DIM, HEAD_DIM), F32),
                   jax.ShapeDtypeStruct((DEPTH, rows, CONV_WIDTH), F32)],
        scratch_shapes=[pltpu.VMEM((tm, PROJ_WIDTH), F32),
                        pltpu.VMEM((tm, HGRN_WIDTH), F32), pltpu.VMEM((tm, HGRN_WIDTH), F32),
                        pltpu.VMEM((tm, HGRN_WIDTH), F32), pltpu.VMEM((tm, HGRN_WIDTH), F32),
                        pltpu.VMEM((tm, HGRN_WIDTH), F32),
                        pltpu.VMEM((tm, HGRN_WIDTH), BF16), pltpu.VMEM((tm, HGRN_WIDTH), BF16),
                        pltpu.VMEM((tm, HGRN_WIDTH), BF16), pltpu.VMEM((rows, D_MODEL), F32)],
        compiler_params=pltpu.CompilerParams(dimension_semantics=("arbitrary", "arbitrary"),
                                             vmem_limit_bytes=V7X_VMEM_LIMIT),
        name="sample_layers",
    )(x, state, ext, *weights, vecs, lv, tri)


def kernel(x_prompt, x_sample, state_hgrn, state_conv, w_in, lb_logits, conv_w, onorm_g, w_out,
           ln1_g, ln1_b, w_ff1, w_ff2, ln2_g, ln2_b):
    n_seq, dec_seq, _ = x_sample.shape
    p = jax.nn.softmax(lb_logits.astype(F32), axis=0)
    cum = jnp.cumsum(p, axis=0)
    lb = cum - cum[0:1]
    zeros_half = jnp.zeros((DEPTH, CONV_WIDTH), F32)
    vecs = jnp.stack([
        ln1_g, ln1_b, ln2_g, ln2_b,
        jnp.concatenate([jnp.log(lb), jnp.log1p(-lb)], axis=-1),
        jnp.concatenate([1.0 - lb, onorm_g], axis=-1),
        jnp.concatenate([conv_w[:, 0], conv_w[:, 1]], axis=-1),
        jnp.concatenate([conv_w[:, 2], zeros_half], axis=-1)], axis=1).astype(F32)
    weights = (w_in.astype(BF16), w_out.astype(BF16), w_ff1.astype(BF16), w_ff2.astype(BF16))
    ext = jnp.pad(state_conv, ((0, 0), (0, 0), (0, dec_seq - (CONV_K - 1)), (0, 0)))
    ext = ext.reshape(DEPTH, n_seq * dec_seq, CONV_WIDTH)

    yp = x_prompt
    ys = x_sample.reshape(n_seq * dec_seq, D_MODEL)
    hp, cp = [], []
    for layer in range(DEPTH):
        yp, hgrn_p, conv_p = _prompt_layer(layer, yp, weights, vecs)
        hp.append(hgrn_p)
        cp.append(conv_p)
    ys, hgrn_s, u_s = _sample_layers(ys, state_hgrn, ext, weights, vecs)
    conv_s = u_s.reshape(DEPTH, n_seq, dec_seq, CONV_WIDTH)[:, :, dec_seq - (CONV_K - 1):]
    return (yp, ys.reshape(n_seq, dec_seq, D_MODEL), jnp.stack(hp), jnp.stack(cp), hgrn_s, conv_s)
```

```python
import functools
import math

import numpy as np
import jax
import jax.numpy as jnp
from jax import lax
from jax.experimental import pallas as pl
from jax.experimental.pallas import tpu as pltpu

F32 = jnp.float32
BF16 = jnp.bfloat16

D_MODEL = 1024
DEPTH = 2
HGRN_WIDTH = 512
CONV_WIDTH = 512
HEAD_DIM = 128
N_HEADS = HGRN_WIDTH // HEAD_DIM
CONV_K = 3
D_FF = 4 * D_MODEL
FF_CHUNK = 1024
N_FF_CHUNKS = D_FF // FF_CHUNK
ALPHA = float((2 * DEPTH) ** 0.25)
LN_EPS = 1e-5
RMS_EPS = 1e-6

HGRN_COLS = 4 * HGRN_WIDTH
CONV_COLS = 3 * CONV_WIDTH
PROJ_WIDTH = HGRN_COLS + CONV_COLS
OFF_Q, OFF_F, OFF_I, OFF_G = 0, HGRN_WIDTH, 2 * HGRN_WIDTH, 3 * HGRN_WIDTH
OFF_B, OFF_C, OFF_H = 0, CONV_WIDTH, 2 * CONV_WIDTH

LOG2_E = math.log2(math.e)
DIAG = 128
CUMSUM_BLOCK = 256
PROMPT_TILE = 512
SAMPLE_SEQS = 16
V7X_VMEM_LIMIT = 60 * 1024 * 1024

ROW_LN1G, ROW_LN1B, ROW_LN2G, ROW_LN2B, ROW_LOGLB, ROW_GATE, ROW_CW01, ROW_CW2 = range(8)


def _dot(a, b):
    return jnp.dot(a, b, preferred_element_type=F32)


def _dot_nt(a, b):
    return lax.dot_general(a, b, (((1,), (1,)), ((), ())), preferred_element_type=F32)


def _dot_tn(a, b):
    return lax.dot_general(a, b, (((0,), (0,)), ((), ())), preferred_element_type=F32)


def _sigmoid(x):
    return 1.0 / (1.0 + jnp.exp(-x))


def _layer_norm(y, g, b):
    mu = jnp.mean(y, axis=-1, keepdims=True)
    yc = y - mu
    var = jnp.mean(yc * yc, axis=-1, keepdims=True)
    return yc * lax.rsqrt(var + LN_EPS) * g + b


def _head_cols(h, off=0):
    return slice(off + h * HEAD_DIM, off + (h + 1) * HEAD_DIM)


def _conv_gate(xb, win_ref, vec_ref, proj_scr, shifted):
    proj_scr[:, 0:CONV_COLS] = _dot(xb, win_ref[:, HGRN_COLS:PROJ_WIDTH])
    u = proj_scr[:, OFF_C:OFF_C + CONV_WIDTH] * proj_scr[:, OFF_H:OFF_H + CONV_WIDTH]
    u1, u2 = shifted(u)
    w0 = vec_ref[ROW_CW01:ROW_CW01 + 1, 0:CONV_WIDTH]
    w1 = vec_ref[ROW_CW01:ROW_CW01 + 1, CONV_WIDTH:2 * CONV_WIDTH]
    w2 = vec_ref[ROW_CW2:ROW_CW2 + 1, 0:CONV_WIDTH]
    return proj_scr[:, OFF_B:OFF_B + CONV_WIDTH] * (w0 * u2 + w1 * u1 + w2 * u), u


def _hgrn_gates(xb, win_ref, vec_ref, tri_ref, proj_scr, kk_scr, b_scr, carry):
    tm = proj_scr.shape[0]
    proj_scr[...] = _dot(xb, win_ref[:, 0:HGRN_COLS])
    qp = proj_scr[:, OFF_Q:OFF_Q + HGRN_WIDTH]
    proj_scr[:, OFF_Q:OFF_Q + HGRN_WIDTH] = qp * _sigmoid(qp)
    z = proj_scr[:, OFF_F:OFF_F + HGRN_WIDTH]
    log_lb = vec_ref[ROW_LOGLB:ROW_LOGLB + 1, 0:HGRN_WIDTH]
    log_1mlb = vec_ref[ROW_LOGLB:ROW_LOGLB + 1, HGRN_WIDTH:2 * HGRN_WIDTH]
    one_m_lb = vec_ref[ROW_GATE:ROW_GATE + 1, 0:HGRN_WIDTH]
    e = jnp.exp(-jnp.abs(z))
    log_sig = jnp.minimum(z, 0.0) - jnp.log1p(e)
    c = log_1mlb + log_sig
    lf = LOG2_E * (jnp.maximum(log_lb, c) + jnp.log1p(jnp.exp(-jnp.abs(log_lb - c))))
    proj_scr[:, OFF_F:OFF_F + HGRN_WIDTH] = lf
    kk_scr[...] = one_m_lb * jnp.where(z >= 0, e, 1.0) / (1.0 + e)
    hi = lf.astype(BF16)
    r1 = lf - hi.astype(F32)
    mid = r1.astype(BF16)
    lo = (r1 - mid.astype(F32)).astype(BF16)
    tri = tri_ref[...]
    tb = tri.shape[0]
    for r0 in range(0, tm, tb):
        rows = slice(r0, r0 + tb)
        bb = _dot(tri, hi[rows]) + _dot(tri, mid[rows]) + _dot(tri, lo[rows])
        if carry and r0 > 0:
            bb = bb + b_scr[r0 - 1:r0, :]
        b_scr[rows, :] = bb


def _level_operand(h, half, tm, proj_scr, kk_scr, b_scr):
    hs = _head_cols(h)
    qs = _head_cols(h, OFF_Q)
    if half >= 8:
        blk = 2 * half
        pieces = []
        for r0 in range(0, tm, blk):
            bm = b_scr[r0 + half - 1:r0 + half, hs]
            pieces.append(kk_scr[r0:r0 + half, hs] * jnp.exp2(bm - b_scr[r0:r0 + half, hs]))
            pieces.append(proj_scr[r0 + half:r0 + blk, qs] * jnp.exp2(b_scr[r0 + half:r0 + blk, hs] - bm))
        return jnp.concatenate(pieces, axis=0).astype(BF16)
    row = lax.broadcasted_iota(jnp.int32, (tm, HEAD_DIM), 0)
    first = (row & half) == 0
    lfh = proj_scr[:, _head_cols(h, OFF_F)]
    if half == 4:
        bh = b_scr[:, hs]
        bm = jnp.concatenate([jnp.broadcast_to(b_scr[r0 + 3:r0 + 4, hs], (8, HEAD_DIM))
                              for r0 in range(0, tm, 8)], axis=0)
        arg = jnp.where(first, bm - bh, bh - bm)
    elif half == 2:
        r = row & 3
        lf_next = pltpu.roll(lfh, tm - 1, 0)
        lf_prev = pltpu.roll(lfh, 1, 0)
        arg = jnp.where(r == 0, lf_next, jnp.where(r == 1, 0.0, jnp.where(r == 2, lfh, lfh + lf_prev)))
    else:
        arg = jnp.where(first, 0.0, lfh)
    return (jnp.where(first, kk_scr[:, hs], proj_scr[:, qs]) * jnp.exp2(arg)).astype(BF16)


def _intra_head(h, seg, tm, proj_scr, kk_scr, b_scr, lv_ref):
    vh = proj_scr[:, _head_cols(h, OFF_I)]
    n_diag = tm // DIAG
    lv = lv_ref[...]
    diag = [None] * n_diag
    cross = {}
    half = seg // 2
    while half >= 1:
        xl = _level_operand(h, half, tm, proj_scr, kk_scr, b_scr)
        if half >= DIAG:
            for r0 in range(0, tm, 2 * half):
                for tq in range(r0 + half, r0 + 2 * half, DIAG):
                    for tk in range(r0, r0 + half, DIAG):
                        cross[(tq // DIAG, tk // DIAG)] = _dot_nt(xl[tq:tq + DIAG], xl[tk:tk + DIAG])
        else:
            lg = int(math.log2(half))
            for d in range(n_diag):
                xd = xl[d * DIAG:(d + 1) * DIAG]
                diag[d] = jnp.where(lv == lg, _dot_nt(xd, xd), 0.0 if diag[d] is None else diag[d])
        half //= 2
    vb = vh.astype(BF16)
    outs = []
    for d in range(n_diag):
        p = jnp.concatenate([cross[(d, e)] for e in range(d)] + [diag[d]], axis=1).astype(BF16)
        outs.append(_dot(p, vb[0:(d + 1) * DIAG]))
    dg = jnp.sum(proj_scr[:, _head_cols(h, OFF_Q)] * kk_scr[:, _head_cols(h)], axis=-1, keepdims=True)
    return jnp.concatenate(outs, axis=0) + dg * vh


def _gated_rms(h, o, proj_scr, vec_ref):
    g = proj_scr[:, _head_cols(h, OFF_G)]
    onorm = vec_ref[ROW_GATE:ROW_GATE + 1, _head_cols(h, HGRN_WIDTH)]
    o = o * lax.rsqrt(jnp.mean(o * o, axis=-1, keepdims=True) + RMS_EPS)
    return o * onorm * (g * _sigmoid(g))


def _mix_norm(x, mixin, wout_ref, vec_ref):
    mix = _dot(mixin, wout_ref[...])
    return _layer_norm(ALPHA * x + mix, vec_ref[ROW_LN1G:ROW_LN1G + 1, :], vec_ref[ROW_LN1B:ROW_LN1B + 1, :])


def _ffn_chunk(c, x1b, w1_ref, w2_ref):
    cs = slice(c * FF_CHUNK, (c + 1) * FF_CHUNK)
    hid = jnp.maximum(_dot(x1b, w1_ref[:, cs]), 0.0)
    return _dot((hid * hid).astype(BF16), w2_ref[cs, :])


def _ffn_norm(x1, ffn, vec_ref):
    return _layer_norm(ALPHA * x1 + ffn, vec_ref[ROW_LN2G:ROW_LN2G + 1, :], vec_ref[ROW_LN2B:ROW_LN2B + 1, :])


def _prompt_kernel(x_ref, win_ref, wout_ref, w1_ref, w2_ref, vec_ref, lv_ref, tri_ref,
                   y_ref, hst_ref, cst_ref,
                   proj_scr, kk_scr, b_scr, u_scr, mix_scr, x1_scr, x1b_scr,
                   *, n_tiles, tiles_per_seq):
    assert N_FF_CHUNKS == N_HEADS
    tm = PROMPT_TILE
    i = pl.program_id(0)
    live = i < n_tiles
    j = jnp.minimum(i, n_tiles - 1) % tiles_per_seq

    @pl.when(i == 0)
    def _():
        x1_scr[...] = jnp.zeros(x1_scr.shape, F32)
        x1b_scr[...] = jnp.zeros(x1b_scr.shape, BF16)

    @pl.when(jnp.logical_and(j == 0, live))
    def _():
        hst_ref[...] = jnp.zeros(hst_ref.shape, F32)
        u_scr[0:8, :] = jnp.zeros((8, CONV_WIDTH), F32)

    x = x_ref[0]
    xb = x.astype(BF16)

    def shifted(u):
        u_scr[8:8 + tm, :] = u
        return u_scr[7:7 + tm, :], u_scr[6:6 + tm, :]

    yc, _ = _conv_gate(xb, win_ref, vec_ref, proj_scr, shifted)
    mix_scr[:, HGRN_WIDTH:HGRN_WIDTH + CONV_WIDTH] = yc.astype(BF16)
    tail = u_scr[tm:tm + 8, :]
    u_scr[0:8, :] = tail
    cst_ref[0] = tail[6:8, :]

    _hgrn_gates(xb, win_ref, vec_ref, tri_ref, proj_scr, kk_scr, b_scr, carry=True)

    for h in range(N_HEADS):
        ffn = _ffn_chunk(h, x1b_scr[...], w1_ref, w2_ref)
        y_ref[0] = ffn if h == 0 else y_ref[0] + ffn
        hs = _head_cols(h)
        o = _intra_head(h, tm, tm, proj_scr, kk_scr, b_scr, lv_ref)
        qh = proj_scr[:, _head_cols(h, OFF_Q)]
        kh = kk_scr[:, hs]
        bh = b_scr[:, hs]
        vh = proj_scr[:, _head_cols(h, OFF_I)]
        s_prev = hst_ref[0, h]
        o = o + _dot((qh * jnp.exp2(bh)).astype(BF16), s_prev.astype(BF16))
        b_last = b_scr[tm - 1:tm, hs]
        kd = (kh * jnp.exp2(b_last - bh)).astype(BF16)
        upd = _dot_tn(kd, vh.astype(BF16))
        a_col = jnp.exp2(jnp.transpose(b_scr[tm - 8:tm, hs])[:, 7:8])
        hst_ref[0, h] = jnp.where(live, a_col * s_prev + upd, s_prev)
        mix_scr[:, hs] = _gated_rms(h, o, proj_scr, vec_ref).astype(BF16)

    y_ref[0] = _ffn_norm(x1_scr[...], y_ref[0], vec_ref)
    x1 = _mix_norm(x, mix_scr[...], wout_ref, vec_ref)
    x1_scr[...] = x1
    x1b_scr[...] = x1.astype(BF16)


def _sample_kernel(x_ref, sin_ref, ext_ref, win_ref, wout_ref, w1_ref, w2_ref, vec_ref, lv_ref, tri_ref,
                   y_ref, sout_ref, u_ref,
                   proj_scr, kk_scr, b_scr, o_scr, qe_scr, kd_scr, vb_scr, ys_scr):
    seg = 8
    tm = SAMPLE_SEQS * seg
    rows = pl.ds(pl.multiple_of(pl.program_id(1) * tm, tm), tm)

    @pl.when(pl.program_id(0) == 0)
    def _():
        ys_scr[rows, :] = x_ref[...]

    x = ys_scr[rows, :]
    xb = x.astype(BF16)

    def shifted(u):
        ext = ext_ref[...]
        tmod = lax.broadcasted_iota(jnp.int32, (tm, CONV_WIDTH), 0) & (seg - 1)
        u1 = jnp.where(tmod == 0, pltpu.roll(ext, tm - 1, 0), pltpu.roll(u, 1, 0))
        u2 = jnp.where(tmod < 2, ext, pltpu.roll(u, 2, 0))
        return u1, u2

    yc, u = _conv_gate(xb, win_ref, vec_ref, proj_scr, shifted)
    u_ref[...] = u
    yc = yc.astype(BF16)

    _hgrn_gates(xb, win_ref, vec_ref, tri_ref, proj_scr, kk_scr, b_scr, carry=False)

    for h in range(N_HEADS):
        hs = _head_cols(h)
        o_scr[:, hs] = _intra_head(h, seg, tm, proj_scr, kk_scr, b_scr, lv_ref)
        bh = b_scr[:, hs]
        b_last = jnp.concatenate([jnp.broadcast_to(b_scr[r0 + seg - 1:r0 + seg, hs], (seg, HEAD_DIM))
                                  for r0 in range(0, tm, seg)], axis=0)
        qe_scr[:, hs] = (proj_scr[:, _head_cols(h, OFF_Q)] * jnp.exp2(bh)).astype(BF16)
        kd_scr[:, hs] = (kk_scr[:, hs] * jnp.exp2(b_last - bh)).astype(BF16)
    vb_scr[...] = proj_scr[:, OFF_I:OFF_I + HGRN_WIDTH].astype(BF16)

    top = lax.broadcasted_iota(jnp.int32, (2 * seg, HEAD_DIM), 0) < seg

    def pair_body(p, carry):
        r0 = pl.multiple_of(p * 2 * seg, 2 * seg)
        for h in range(N_HEADS):
            hs = _head_cols(h)
            qe = qe_scr[pl.ds(r0, 2 * seg), hs]
            kd = kd_scr[pl.ds(r0, 2 * seg), hs]
            vv = vb_scr[pl.ds(r0, 2 * seg), hs]
            bt = jnp.transpose(b_scr[pl.ds(r0, 2 * seg), hs])
            s_a = sin_ref[2 * p, h]
            s_b = sin_ref[2 * p + 1, h]
            o_a = _dot(qe, s_a.astype(BF16))
            o_b = _dot(qe, s_b.astype(BF16))
            o_scr[pl.ds(r0, 2 * seg), hs] = o_scr[pl.ds(r0, 2 * seg), hs] + jnp.where(top, o_a, o_b)
            zero = jnp.zeros_like(kd)
            sout_ref[2 * p, h] = jnp.exp2(bt[:, seg - 1:seg]) * s_a + _dot_tn(jnp.where(top, kd, zero), vv)
            sout_ref[2 * p + 1, h] = jnp.exp2(bt[:, 2 * seg - 1:2 * seg]) * s_b + _dot_tn(jnp.where(top, zero, kd), vv)
        return carry

    lax.fori_loop(0, SAMPLE_SEQS // 2, pair_body, 0)

    mixin = jnp.concatenate([_gated_rms(h, o_scr[:, _head_cols(h)], proj_scr, vec_ref).astype(BF16)
                             for h in range(N_HEADS)] + [yc], axis=-1)
    x1 = _mix_norm(x, mixin, wout_ref, vec_ref)
    x1b = x1.astype(BF16)
    ffn = _ffn_chunk(0, x1b, w1_ref, w2_ref)
    for c in range(1, N_FF_CHUNKS):
        ffn = ffn + _ffn_chunk(c, x1b, w1_ref, w2_ref)
    y = _ffn_norm(x1, ffn, vec_ref)
    ys_scr[rows, :] = y
    y_ref[...] = y


def _level_table(tm, seg):
    t = np.arange(tm)[:, None]
    s = np.arange(tm)[None, :]
    x = np.maximum(t ^ s, 1)
    lv = np.floor(np.log2(x)).astype(np.int32)
    ok = (s < t) & ((t // seg) == (s // seg))
    return np.where(ok, lv, -1).astype(np.int32)


def _tri_table(tm, seg):
    t = np.arange(tm)[:, None]
    s = np.arange(tm)[None, :]
    return ((s <= t) & ((t // seg) == (s // seg))).astype(np.float32)


def _const_spec(shape):
    nd = len(shape)
    return pl.BlockSpec(shape, lambda *_: (0,) * nd, pipeline_mode=pl.Buffered(1))


def _weight_specs(layer_of):
    def wspec(k, n):
        return pl.BlockSpec((None, k, n), lambda *g: (layer_of(*g), 0, 0), pipeline_mode=pl.Buffered(1))
    return [wspec(D_MODEL, PROJ_WIDTH), wspec(D_MODEL, D_MODEL), wspec(D_MODEL, D_FF), wspec(D_FF, D_MODEL),
            wspec(8, D_MODEL)]


def _prompt_layer(layer, x, weights, vecs):
    bsz, seq, _ = x.shape
    tm = PROMPT_TILE
    tb = min(tm, CUMSUM_BLOCK)
    tiles_per_seq = seq // tm
    n_tiles = bsz * tiles_per_seq
    lv = jnp.asarray(_level_table(DIAG, DIAG))
    tri = jnp.asarray(_tri_table(tb, tb), dtype=BF16)

    def cur(i):
        t = jnp.minimum(i, n_tiles - 1)
        return t // tiles_per_seq, t % tiles_per_seq

    def prev(i):
        t = jnp.maximum(i - 1, 0)
        return t // tiles_per_seq, t % tiles_per_seq

    return pl.pallas_call(
        functools.partial(_prompt_kernel, n_tiles=n_tiles, tiles_per_seq=tiles_per_seq),
        grid=(n_tiles + 1,),
        in_specs=[pl.BlockSpec((1, tm, D_MODEL), lambda i: (*cur(i), 0))] + _weight_specs(lambda i: layer)
        + [_const_spec((DIAG, DIAG)), _const_spec((tb, tb))],
        out_specs=[pl.BlockSpec((1, tm, D_MODEL), lambda i: (*prev(i), 0)),
                   pl.BlockSpec((1, N_HEADS, HEAD_DIM, HEAD_DIM), lambda i: (cur(i)[0], 0, 0, 0)),
                   pl.BlockSpec((1, CONV_K - 1, CONV_WIDTH), lambda i: (cur(i)[0], 0, 0))],
        out_shape=[jax.ShapeDtypeStruct((bsz, seq, D_MODEL), F32),
                   jax.ShapeDtypeStruct((bsz, N_HEADS, HEAD_DIM, HEAD_DIM), F32),
                   jax.ShapeDtypeStruct((bsz, CONV_K - 1, CONV_WIDTH), F32)],
        scratch_shapes=[pltpu.VMEM((tm, HGRN_COLS), F32),
                        pltpu.VMEM((tm, HGRN_WIDTH), F32), pltpu.VMEM((tm, HGRN_WIDTH), F32),
                        pltpu.VMEM((tm + 8, CONV_WIDTH), F32), pltpu.VMEM((tm, D_MODEL), BF16),
                        pltpu.VMEM((tm, D_MODEL), F32), pltpu.VMEM((tm, D_MODEL), BF16)],
        compiler_params=pltpu.CompilerParams(dimension_semantics=("arbitrary",),
                                             vmem_limit_bytes=V7X_VMEM_LIMIT),
        name=f"prompt_layer{layer}",
    )(x, *weights, vecs, lv, tri)


def _sample_layers(x, state, ext, weights, vecs):
    rows = x.shape[0]
    seg = 8
    tm = SAMPLE_SEQS * seg
    n_seq = rows // seg
    n_steps = n_seq // SAMPLE_SEQS
    lv = jnp.asarray(_level_table(DIAG, seg))
    tri = jnp.asarray(_tri_table(tm, seg), dtype=BF16)
    st_spec = pl.BlockSpec((None, SAMPLE_SEQS, N_HEADS, HEAD_DIM, HEAD_DIM), lambda l, i: (l, i, 0, 0, 0))
    u_spec = pl.BlockSpec((None, tm, CONV_WIDTH), lambda l, i: (l, i, 0))
    return pl.pallas_call(
        _sample_kernel,
        grid=(DEPTH, n_steps),
        in_specs=[pl.BlockSpec((tm, D_MODEL), lambda l, i: (i * (1 - l) + (n_steps - 1) * l, 0)), st_spec, u_spec]
        + _weight_specs(lambda l, i: l) + [_const_spec((DIAG, DIAG)), _const_spec((tm, tm))],
        out_specs=[pl.BlockSpec((tm, D_MODEL), lambda l, i: (i * l, 0)), st_spec, u_spec],
        out_shape=[jax.ShapeDtypeStruct((rows, D_MODEL), F32),
                   jax.ShapeDtypeStruct((DEPTH, n_seq, N_HEADS, HEAD_DIM, HEAD_DIM), F32),
                   jax.ShapeDtypeStruct((DEPTH, rows, CONV_WIDTH), F32)],
        scratch_shapes=[pltpu.VMEM((tm, HGRN_COLS), F32),
                        pltpu.VMEM((tm, HGRN_WIDTH), F32), pltpu.VMEM((tm, HGRN_WIDTH), F32),
                        pltpu.VMEM((tm, HGRN_WIDTH), F32),
                        pltpu.VMEM((tm, HGRN_WIDTH), BF16), pltpu.VMEM((tm, HGRN_WIDTH), BF16),
                        pltpu.VMEM((tm, HGRN_WIDTH), BF16), pltpu.VMEM((rows, D_MODEL), F32)],
        compiler_params=pltpu.CompilerParams(dimension_semantics=("arbitrary", "arbitrary"),
                                             vmem_limit_bytes=V7X_VMEM_LIMIT),
        name="sample_layers",
    )(x, state, ext, *weights, vecs, lv, tri)


def kernel(x_prompt, x_sample, state_hgrn, state_conv, w_in, lb_logits, conv_w, onorm_g, w_out,
           ln1_g, ln1_b, w_ff1, w_ff2, ln2_g, ln2_b):
    n_seq, dec_seq, _ = x_sample.shape
    p = jax.nn.softmax(lb_logits.astype(F32), axis=0)
    cum = jnp.cumsum(p, axis=0)
    lb = cum - cum[0:1]
    zeros_half = jnp.zeros((DEPTH, CONV_WIDTH), F32)
    vecs = jnp.stack([
        ln1_g, ln1_b, ln2_g, ln2_b,
        jnp.concatenate([jnp.log(lb), jnp.log1p(-lb)], axis=-1),
        jnp.concatenate([1.0 - lb, onorm_g], axis=-1),
        jnp.concatenate([conv_w[:, 0], conv_w[:, 1]], axis=-1),
        jnp.concatenate([conv_w[:, 2], zeros_half], axis=-1)], axis=1).astype(F32)
    weights = (w_in.astype(BF16), w_out.astype(BF16), w_ff1.astype(BF16), w_ff2.astype(BF16))
    ext = jnp.pad(state_conv, ((0, 0), (0, 0), (0, dec_seq - (CONV_K - 1)), (0, 0)))
    ext = ext.reshape(DEPTH, n_seq * dec_seq, CONV_WIDTH)

    yp = x_prompt
    ys = x_sample.reshape(n_seq * dec_seq, D_MODEL)
    hp, cp = [], []
    for layer in range(DEPTH):
        yp, hgrn_p, conv_p = _prompt_layer(layer, yp, weights, vecs)
        hp.append(hgrn_p)
        cp.append(conv_p)
    ys, hgrn_s, u_s = _sample_layers(ys, state_hgrn, ext, weights, vecs)
    conv_s = u_s.reshape(DEPTH, n_seq, dec_seq, CONV_WIDTH)[:, :, dec_seq - (CONV_K - 1):]
    return (yp, ys.reshape(n_seq, dec_seq, D_MODEL), jnp.stack(hp), jnp.stack(cp), hgrn_s, conv_s)
```

```python
import functools
import math

import numpy as np
import jax
import jax.numpy as jnp
from jax import lax
from jax.experimental import pallas as pl
from jax.experimental.pallas import tpu as pltpu

F32 = jnp.float32
BF16 = jnp.bfloat16

D_MODEL = 1024
DEPTH = 2
HGRN_WIDTH = 512
CONV_WIDTH = 512
HEAD_DIM = 128
N_HEADS = HGRN_WIDTH // HEAD_DIM
CONV_K = 3
D_FF = 4 * D_MODEL
FF_CHUNK = 1024
N_FF_CHUNKS = D_FF // FF_CHUNK
ALPHA = float((2 * DEPTH) ** 0.25)
LN_EPS = 1e-5
RMS_EPS = 1e-6

HGRN_COLS = 4 * HGRN_WIDTH
CONV_COLS = 3 * CONV_WIDTH
PROJ_WIDTH = HGRN_COLS + CONV_COLS
OFF_Q, OFF_F, OFF_I, OFF_G = 0, HGRN_WIDTH, 2 * HGRN_WIDTH, 3 * HGRN_WIDTH
OFF_B, OFF_C, OFF_H = 0, CONV_WIDTH, 2 * CONV_WIDTH

LOG2_E = math.log2(math.e)
DIAG = 128
LANE_PAD = 128
CUMSUM_BLOCK = 256
PROMPT_TILE = 256
SAMPLE_SEQS = 16
V7X_VMEM_LIMIT = 60 * 1024 * 1024

ROW_LN1G, ROW_LN1B, ROW_LN2G, ROW_LN2B, ROW_LOGLB, ROW_GATE, ROW_CW01, ROW_CW2 = range(8)


def _dot(a, b):
    return jnp.dot(a, b, preferred_element_type=F32)


def _dot_nt(a, b):
    return lax.dot_general(a, b, (((1,), (1,)), ((), ())), preferred_element_type=F32)


def _dot_tn(a, b):
    return lax.dot_general(a, b, (((0,), (0,)), ((), ())), preferred_element_type=F32)


def _sigmoid(x):
    return 1.0 / (1.0 + jnp.exp(-x))


def _layer_norm(y, g, b):
    mu = jnp.mean(y, axis=-1, keepdims=True)
    yc = y - mu
    var = jnp.mean(yc * yc, axis=-1, keepdims=True)
    return yc * lax.rsqrt(var + LN_EPS) * g + b


def _head_cols(h, off=0):
    return slice(off + h * HEAD_DIM, off + (h + 1) * HEAD_DIM)


def _conv_gate(xb, win_ref, vec_ref, proj_scr, shifted):
    proj_scr[:, 0:CONV_COLS] = _dot(xb, win_ref[:, HGRN_COLS:PROJ_WIDTH])
    u = proj_scr[:, OFF_C:OFF_C + CONV_WIDTH] * proj_scr[:, OFF_H:OFF_H + CONV_WIDTH]
    u1, u2 = shifted(u)
    w0 = vec_ref[ROW_CW01:ROW_CW01 + 1, 0:CONV_WIDTH]
    w1 = vec_ref[ROW_CW01:ROW_CW01 + 1, CONV_WIDTH:2 * CONV_WIDTH]
    w2 = vec_ref[ROW_CW2:ROW_CW2 + 1, 0:CONV_WIDTH]
    return proj_scr[:, OFF_B:OFF_B + CONV_WIDTH] * (w0 * u2 + w1 * u1 + w2 * u), u


def _hgrn_gates(xb, win_ref, vec_ref, tri_ref, proj_scr, kk_scr, b_scr, carry):
    tm = proj_scr.shape[0]
    proj_scr[...] = _dot(xb, win_ref[:, 0:HGRN_COLS])
    qp = proj_scr[:, OFF_Q:OFF_Q + HGRN_WIDTH]
    proj_scr[:, OFF_Q:OFF_Q + HGRN_WIDTH] = qp * _sigmoid(qp)
    z = proj_scr[:, OFF_F:OFF_F + HGRN_WIDTH]
    log_lb = vec_ref[ROW_LOGLB:ROW_LOGLB + 1, 0:HGRN_WIDTH]
    log_1mlb = vec_ref[ROW_LOGLB:ROW_LOGLB + 1, HGRN_WIDTH:2 * HGRN_WIDTH]
    one_m_lb = vec_ref[ROW_GATE:ROW_GATE + 1, 0:HGRN_WIDTH]
    e = jnp.exp(-jnp.abs(z))
    log_sig = jnp.minimum(z, 0.0) - jnp.log1p(e)
    c = log_1mlb + log_sig
    lf = LOG2_E * (jnp.maximum(log_lb, c) + jnp.log1p(jnp.exp(-jnp.abs(log_lb - c))))
    proj_scr[:, OFF_F:OFF_F + HGRN_WIDTH] = lf
    kk_scr[...] = one_m_lb * jnp.where(z >= 0, e, 1.0) / (1.0 + e)
    hi = lf.astype(BF16)
    r1 = lf - hi.astype(F32)
    mid = r1.astype(BF16)
    lo = (r1 - mid.astype(F32)).astype(BF16)
    tri = tri_ref[...]
    tb = tri.shape[0]
    for r0 in range(0, tm, tb):
        rows = slice(r0, r0 + tb)
        bb = _dot(tri, hi[rows]) + _dot(tri, mid[rows]) + _dot(tri, lo[rows])
        if carry and r0 > 0:
            bb = bb + b_scr[r0 - 1:r0, :]
        b_scr[rows, :] = bb


def _level_operand(h, half, tm, proj_scr, kk_scr, b_scr):
    hs = _head_cols(h)
    qs = _head_cols(h, OFF_Q)
    if half >= 8:
        blk = 2 * half
        pieces = []
        for r0 in range(0, tm, blk):
            bm = b_scr[r0 + half - 1:r0 + half, hs]
            pieces.append(kk_scr[r0:r0 + half, hs] * jnp.exp2(bm - b_scr[r0:r0 + half, hs]))
            pieces.append(proj_scr[r0 + half:r0 + blk, qs] * jnp.exp2(b_scr[r0 + half:r0 + blk, hs] - bm))
        return jnp.concatenate(pieces, axis=0).astype(BF16)
    row = lax.broadcasted_iota(jnp.int32, (tm, HEAD_DIM), 0)
    first = (row & half) == 0
    lfh = proj_scr[:, _head_cols(h, OFF_F)]
    if half == 4:
        bh = b_scr[:, hs]
        bm = jnp.concatenate([jnp.broadcast_to(b_scr[r0 + 3:r0 + 4, hs], (8, HEAD_DIM))
                              for r0 in range(0, tm, 8)], axis=0)
        arg = jnp.where(first, bm - bh, bh - bm)
    elif half == 2:
        r = row & 3
        lf_next = pltpu.roll(lfh, tm - 1, 0)
        lf_prev = pltpu.roll(lfh, 1, 0)
        arg = jnp.where(r == 0, lf_next, jnp.where(r == 1, 0.0, jnp.where(r == 2, lfh, lfh + lf_prev)))
    else:
        arg = jnp.where(first, 0.0, lfh)
    return (jnp.where(first, kk_scr[:, hs], proj_scr[:, qs]) * jnp.exp2(arg)).astype(BF16)


def _intra_head(h, seg, tm, proj_scr, kk_scr, b_scr, lv_ref):
    vh = proj_scr[:, _head_cols(h, OFF_I)]
    n_diag = tm // DIAG
    lv = lv_ref[...]
    diag = [None] * n_diag
    cross = {}
    half = seg // 2
    while half >= 1:
        xl = _level_operand(h, half, tm, proj_scr, kk_scr, b_scr)
        if half >= DIAG:
            for r0 in range(0, tm, 2 * half):
                for tq in range(r0 + half, r0 + 2 * half, DIAG):
                    for tk in range(r0, r0 + half, DIAG):
                        cross[(tq // DIAG, tk // DIAG)] = _dot_nt(xl[tq:tq + DIAG], xl[tk:tk + DIAG])
        else:
            lg = int(math.log2(half))
            for d in range(n_diag):
                xd = xl[d * DIAG:(d + 1) * DIAG]
                diag[d] = jnp.where(lv == lg, _dot_nt(xd, xd), 0.0 if diag[d] is None else diag[d])
        half //= 2
    vb = vh.astype(BF16)
    outs = []
    for d in range(n_diag):
        p = jnp.concatenate([cross[(d, e)] for e in range(d)] + [diag[d]], axis=1).astype(BF16)
        outs.append(_dot(p, vb[0:(d + 1) * DIAG]))
    dg = jnp.sum(proj_scr[:, _head_cols(h, OFF_Q)] * kk_scr[:, _head_cols(h)], axis=-1, keepdims=True)
    return jnp.concatenate(outs, axis=0) + dg * vh


def _gated_rms(h, o, proj_scr, vec_ref):
    g = proj_scr[:, _head_cols(h, OFF_G)]
    onorm = vec_ref[ROW_GATE:ROW_GATE + 1, _head_cols(h, HGRN_WIDTH)]
    o = o * lax.rsqrt(jnp.mean(o * o, axis=-1, keepdims=True) + RMS_EPS)
    return o * onorm * (g * _sigmoid(g))


def _mix_norm(x, mixin, wout_ref, vec_ref):
    mix = _dot(mixin, wout_ref[:, 0:D_MODEL])
    return _layer_norm(ALPHA * x + mix, vec_ref[ROW_LN1G:ROW_LN1G + 1, :], vec_ref[ROW_LN1B:ROW_LN1B + 1, :])


def _ffn_chunk(c, x1b, w1_ref, w2_ref):
    cs = slice(c * FF_CHUNK, (c + 1) * FF_CHUNK)
    hid = jnp.maximum(_dot(x1b, w1_ref[:, cs]), 0.0)
    return _dot((hid * hid).astype(BF16), w2_ref[cs, 0:D_MODEL])


def _ffn_norm(x1, ffn, vec_ref):
    return _layer_norm(ALPHA * x1 + ffn, vec_ref[ROW_LN2G:ROW_LN2G + 1, :], vec_ref[ROW_LN2B:ROW_LN2B + 1, :])


def _prompt_kernel(x_ref, win_ref, wout_ref, w1_ref, w2_ref, vec_ref, lv_ref, tri_ref,
                   y_ref, hst_ref, cst_ref,
                   proj_scr, kk_scr, b_scr, u_scr, mix_scr, x1_scr, x1b_scr,
                   *, n_tiles, tiles_per_seq):
    assert N_FF_CHUNKS == N_HEADS
    tm = PROMPT_TILE
    i = pl.program_id(0)
    live = i < n_tiles
    j = jnp.minimum(i, n_tiles - 1) % tiles_per_seq

    @pl.when(i == 0)
    def _():
        x1_scr[...] = jnp.zeros(x1_scr.shape, F32)
        x1b_scr[...] = jnp.zeros(x1b_scr.shape, BF16)

    @pl.when(jnp.logical_and(j == 0, live))
    def _():
        hst_ref[...] = jnp.zeros(hst_ref.shape, F32)
        u_scr[0:8, :] = jnp.zeros((8, CONV_WIDTH), F32)

    x = x_ref[0]
    xb = x.astype(BF16)

    def shifted(u):
        u_scr[8:8 + tm, :] = u
        return u_scr[7:7 + tm, :], u_scr[6:6 + tm, :]

    yc, _ = _conv_gate(xb, win_ref, vec_ref, proj_scr, shifted)
    mix_scr[:, HGRN_WIDTH:HGRN_WIDTH + CONV_WIDTH] = yc.astype(BF16)
    tail = u_scr[tm:tm + 8, :]
    u_scr[0:8, :] = tail
    cst_ref[0] = tail[6:8, :]

    _hgrn_gates(xb, win_ref, vec_ref, tri_ref, proj_scr, kk_scr, b_scr, carry=True)

    for h in range(N_HEADS):
        ffn = _ffn_chunk(h, x1b_scr[...], w1_ref, w2_ref)
        y_ref[0] = ffn if h == 0 else y_ref[0] + ffn
        hs = _head_cols(h)
        o = _intra_head(h, tm, tm, proj_scr, kk_scr, b_scr, lv_ref)
        qh = proj_scr[:, _head_cols(h, OFF_Q)]
        kh = kk_scr[:, hs]
        bh = b_scr[:, hs]
        vh = proj_scr[:, _head_cols(h, OFF_I)]
        s_prev = hst_ref[0, h]
        o = o + _dot((qh * jnp.exp2(bh)).astype(BF16), s_prev.astype(BF16))
        b_last = b_scr[tm - 1:tm, hs]
        kd = (kh * jnp.exp2(b_last - bh)).astype(BF16)
        upd = _dot_tn(kd, vh.astype(BF16))
        a_col = jnp.exp2(jnp.transpose(b_scr[tm - 8:tm, hs])[:, 7:8])
        hst_ref[0, h] = jnp.where(live, a_col * s_prev + upd, s_prev)
        mix_scr[:, hs] = _gated_rms(h, o, proj_scr, vec_ref).astype(BF16)

    y_ref[0] = _ffn_norm(x1_scr[...], y_ref[0], vec_ref)
    x1 = _mix_norm(x, mix_scr[...], wout_ref, vec_ref)
    x1_scr[...] = x1
    x1b_scr[...] = x1.astype(BF16)


def _sample_kernel(x_ref, sin_ref, ext_ref, win_ref, wout_ref, w1_ref, w2_ref, vec_ref, lv_ref, tri_ref,
                   y_ref, sout_ref, u_ref,
                   proj_scr, kk_scr, b_scr, o_scr, qe_scr, kd_scr, vb_scr, ys_scr):
    seg = 8
    tm = SAMPLE_SEQS * seg
    rows = pl.ds(pl.multiple_of(pl.program_id(1) * tm, tm), tm)

    @pl.when(pl.program_id(0) == 0)
    def _():
        ys_scr[rows, :] = x_ref[...]

    x = ys_scr[rows, :]
    xb = x.astype(BF16)

    def shifted(u):
        ext = ext_ref[...]
        tmod = lax.broadcasted_iota(jnp.int32, (tm, CONV_WIDTH), 0) & (seg - 1)
        u1 = jnp.where(tmod == 0, pltpu.roll(ext, tm - 1, 0), pltpu.roll(u, 1, 0))
        u2 = jnp.where(tmod < 2, ext, pltpu.roll(u, 2, 0))
        return u1, u2

    yc, u = _conv_gate(xb, win_ref, vec_ref, proj_scr, shifted)
    u_ref[...] = u
    yc = yc.astype(BF16)

    _hgrn_gates(xb, win_ref, vec_ref, tri_ref, proj_scr, kk_scr, b_scr, carry=False)

    for h in range(N_HEADS):
        hs = _head_cols(h)
        o_scr[:, hs] = _intra_head(h, seg, tm, proj_scr, kk_scr, b_scr, lv_ref)
        bh = b_scr[:, hs]
        b_last = jnp.concatenate([jnp.broadcast_to(b_scr[r0 + seg - 1:r0 + seg, hs], (seg, HEAD_DIM))
                                  for r0 in range(0, tm, seg)], axis=0)
        qe_scr[:, hs] = (proj_scr[:, _head_cols(h, OFF_Q)] * jnp.exp2(bh)).astype(BF16)
        kd_scr[:, hs] = (kk_scr[:, hs] * jnp.exp2(b_last - bh)).astype(BF16)
    vb_scr[...] = proj_scr[:, OFF_I:OFF_I + HGRN_WIDTH].astype(BF16)

    top = lax.broadcasted_iota(jnp.int32, (2 * seg, HEAD_DIM), 0) < seg

    def pair_body(p, carry):
        r0 = pl.multiple_of(p * 2 * seg, 2 * seg)
        for h in range(N_HEADS):
            hs = _head_cols(h)
            qe = qe_scr[pl.ds(r0, 2 * seg), hs]
            kd = kd_scr[pl.ds(r0, 2 * seg), hs]
            vv = vb_scr[pl.ds(r0, 2 * seg), hs]
            bt = jnp.transpose(b_scr[pl.ds(r0, 2 * seg), hs])
            s_a = sin_ref[2 * p, h]
            s_b = sin_ref[2 * p + 1, h]
            o_a = _dot(qe, s_a.astype(BF16))
            o_b = _dot(qe, s_b.astype(BF16))
            o_scr[pl.ds(r0, 2 * seg), hs] = o_scr[pl.ds(r0, 2 * seg), hs] + jnp.where(top, o_a, o_b)
            zero = jnp.zeros_like(kd)
            sout_ref[2 * p, h] = jnp.exp2(bt[:, seg - 1:seg]) * s_a + _dot_tn(jnp.where(top, kd, zero), vv)
            sout_ref[2 * p + 1, h] = jnp.exp2(bt[:, 2 * seg - 1:2 * seg]) * s_b + _dot_tn(jnp.where(top, zero, kd), vv)
        return carry

    lax.fori_loop(0, SAMPLE_SEQS // 2, pair_body, 0)

    mixin = jnp.concatenate([_gated_rms(h, o_scr[:, _head_cols(h)], proj_scr, vec_ref).astype(BF16)
                             for h in range(N_HEADS)] + [yc], axis=-1)
    x1 = _mix_norm(x, mixin, wout_ref, vec_ref)
    x1b = x1.astype(BF16)
    ffn = _ffn_chunk(0, x1b, w1_ref, w2_ref)
    for c in range(1, N_FF_CHUNKS):
        ffn = ffn + _ffn_chunk(c, x1b, w1_ref, w2_ref)
    y = _ffn_norm(x1, ffn, vec_ref)
    ys_scr[rows, :] = y
    y_ref[...] = y


def _level_table(tm, seg):
    t = np.arange(tm)[:, None]
    s = np.arange(tm)[None, :]
    x = np.maximum(t ^ s, 1)
    lv = np.floor(np.log2(x)).astype(np.int32)
    ok = (s < t) & ((t // seg) == (s // seg))
    return np.where(ok, lv, -1).astype(np.int32)


def _tri_table(tm, seg):
    t = np.arange(tm)[:, None]
    s = np.arange(tm)[None, :]
    return ((s <= t) & ((t // seg) == (s // seg))).astype(np.float32)


def _const_spec(shape):
    nd = len(shape)
    return pl.BlockSpec(shape, lambda *_: (0,) * nd, pipeline_mode=pl.Buffered(1))


def _weight_specs(layer_of):
    def wspec(k, n):
        return pl.BlockSpec((None, k, n), lambda *g: (layer_of(*g), 0, 0), pipeline_mode=pl.Buffered(1))
    return [wspec(D_MODEL, PROJ_WIDTH), wspec(D_MODEL, D_MODEL + LANE_PAD), wspec(D_MODEL, D_FF + LANE_PAD),
            wspec(D_FF, D_MODEL + LANE_PAD),
            wspec(8, D_MODEL)]


def _prompt_layer(layer, x, weights, vecs):
    bsz, seq, _ = x.shape
    tm = PROMPT_TILE
    tb = min(tm, CUMSUM_BLOCK)
    tiles_per_seq = seq // tm
    n_tiles = bsz * tiles_per_seq
    lv = jnp.asarray(_level_table(DIAG, DIAG))
    tri = jnp.asarray(_tri_table(tb, tb), dtype=BF16)

    def cur(i):
        t = jnp.minimum(i, n_tiles - 1)
        return t // tiles_per_seq, t % tiles_per_seq

    def prev(i):
        t = jnp.maximum(i - 1, 0)
        return t // tiles_per_seq, t % tiles_per_seq

    return pl.pallas_call(
        functools.partial(_prompt_kernel, n_tiles=n_tiles, tiles_per_seq=tiles_per_seq),
        grid=(n_tiles + 1,),
        in_specs=[pl.BlockSpec((1, tm, D_MODEL), lambda i: (*cur(i), 0))] + _weight_specs(lambda i: layer)
        + [_const_spec((DIAG, DIAG)), _const_spec((tb, tb))],
        out_specs=[pl.BlockSpec((1, tm, D_MODEL), lambda i: (*prev(i), 0)),
                   pl.BlockSpec((1, N_HEADS, HEAD_DIM, HEAD_DIM), lambda i: (cur(i)[0], 0, 0, 0)),
                   pl.BlockSpec((1, CONV_K - 1, CONV_WIDTH), lambda i: (cur(i)[0], 0, 0))],
        out_shape=[jax.ShapeDtypeStruct((bsz, seq, D_MODEL), F32),
                   jax.ShapeDtypeStruct((bsz, N_HEADS, HEAD_DIM, HEAD_DIM), F32),
                   jax.ShapeDtypeStruct((bsz, CONV_K - 1, CONV_WIDTH), F32)],
        scratch_shapes=[pltpu.VMEM((tm, HGRN_COLS), F32),
                        pltpu.VMEM((tm, HGRN_WIDTH), F32), pltpu.VMEM((tm, HGRN_WIDTH), F32),
                        pltpu.VMEM((tm + 8, CONV_WIDTH), F32), pltpu.VMEM((tm, D_MODEL), BF16),
                        pltpu.VMEM((tm, D_MODEL), F32), pltpu.VMEM((tm, D_MODEL), BF16)],
        compiler_params=pltpu.CompilerParams(dimension_semantics=("arbitrary",),
                                             vmem_limit_bytes=V7X_VMEM_LIMIT),
        name=f"prompt_layer{layer}",
    )(x, *weights, vecs, lv, tri)


def _sample_layers(x, state, ext, weights, vecs):
    rows = x.shape[0]
    seg = 8
    tm = SAMPLE_SEQS * seg
    n_seq = rows // seg
    n_steps = n_seq // SAMPLE_SEQS
    lv = jnp.asarray(_level_table(DIAG, seg))
    tri = jnp.asarray(_tri_table(tm, seg), dtype=BF16)
    st_spec = pl.BlockSpec((None, SAMPLE_SEQS, N_HEADS, HEAD_DIM, HEAD_DIM), lambda l, i: (l, i, 0, 0, 0))
    u_spec = pl.BlockSpec((None, tm, CONV_WIDTH), lambda l, i: (l, i, 0))
    return pl.pallas_call(
        _sample_kernel,
        grid=(DEPTH, n_steps),
        in_specs=[pl.BlockSpec((tm, D_MODEL), lambda l, i: (i * (1 - l) + (n_steps - 1) * l, 0)), st_spec, u_spec]
        + _weight_specs(lambda l, i: l) + [_const_spec((DIAG, DIAG)), _const_spec((tm, tm))],
        out_specs=[pl.BlockSpec((tm, D_MODEL), lambda l, i: (i * l, 0)), st_spec, u_spec],
        out_shape=[jax.ShapeDtypeStruct((rows, D_MODEL), F32),
                   jax.ShapeDtypeStruct((DEPTH, n_seq, N_HEADS, HEAD_DIM, HEAD_DIM), F32),
                   jax.ShapeDtypeStruct((DEPTH, rows, CONV_WIDTH), F32)],
        scratch_shapes=[pltpu.VMEM((tm, HGRN_COLS), F32),
                        pltpu.VMEM((tm, HGRN_WIDTH), F32), pltpu.VMEM((tm, HGRN_WIDTH), F32),
                        pltpu.VMEM((tm, HGRN_WIDTH), F32),
                        pltpu.VMEM((tm, HGRN_WIDTH), BF16), pltpu.VMEM((tm, HGRN_WIDTH), BF16),
                        pltpu.VMEM((tm, HGRN_WIDTH), BF16), pltpu.VMEM((rows, D_MODEL), F32)],
        compiler_params=pltpu.CompilerParams(dimension_semantics=("arbitrary", "arbitrary"),
                                             vmem_limit_bytes=V7X_VMEM_LIMIT),
        name="sample_layers",
    )(x, state, ext, *weights, vecs, lv, tri)


def kernel(x_prompt, x_sample, state_hgrn, state_conv, w_in, lb_logits, conv_w, onorm_g, w_out,
           ln1_g, ln1_b, w_ff1, w_ff2, ln2_g, ln2_b):
    n_seq, dec_seq, _ = x_sample.shape
    p = jax.nn.softmax(lb_logits.astype(F32), axis=0)
    cum = jnp.cumsum(p, axis=0)
    lb = cum - cum[0:1]
    zeros_half = jnp.zeros((DEPTH, CONV_WIDTH), F32)
    vecs = jnp.stack([
        ln1_g, ln1_b, ln2_g, ln2_b,
        jnp.concatenate([jnp.log(lb), jnp.log1p(-lb)], axis=-1),
        jnp.concatenate([1.0 - lb, onorm_g], axis=-1),
        jnp.concatenate([conv_w[:, 0], conv_w[:, 1]], axis=-1),
        jnp.concatenate([conv_w[:, 2], zeros_half], axis=-1)], axis=1).astype(F32)
    pad = ((0, 0), (0, 0), (0, LANE_PAD))
    weights = (w_in.astype(BF16), jnp.pad(w_out.astype(BF16), pad), jnp.pad(w_ff1.astype(BF16), pad),
               jnp.pad(w_ff2.astype(BF16), pad))
    ext = jnp.pad(state_conv, ((0, 0), (0, 0), (0, dec_seq - (CONV_K - 1)), (0, 0)))
    ext = ext.reshape(DEPTH, n_seq * dec_seq, CONV_WIDTH)

    yp = x_prompt
    ys = x_sample.reshape(n_seq * dec_seq, D_MODEL)
    hp, cp = [], []
    for layer in range(DEPTH):
        yp, hgrn_p, conv_p = _prompt_layer(layer, yp, weights, vecs)
        hp.append(hgrn_p)
        cp.append(conv_p)
    ys, hgrn_s, u_s = _sample_layers(ys, state_hgrn, ext, weights, vecs)
    conv_s = u_s.reshape(DEPTH, n_seq, dec_seq, CONV_WIDTH)[:, :, dec_seq - (CONV_K - 1):]
    return (yp, ys.reshape(n_seq, dec_seq, D_MODEL), jnp.stack(hp), jnp.stack(cp), hgrn_s, conv_s)
```

```python
import functools
import math

import numpy as np
import jax
import jax.numpy as jnp
from jax import lax
from jax.experimental import pallas as pl
from jax.experimental.pallas import tpu as pltpu

F32 = jnp.float32
BF16 = jnp.bfloat16

D_MODEL = 1024
DEPTH = 2
HGRN_WIDTH = 512
CONV_WIDTH = 512
HEAD_DIM = 128
N_HEADS = HGRN_WIDTH // HEAD_DIM
CONV_K = 3
D_FF = 4 * D_MODEL
PROJ_WIDTH = 4 * HGRN_WIDTH + 3 * CONV_WIDTH
FF_CHUNK = 1024
N_FF_CHUNKS = D_FF // FF_CHUNK
ALPHA = float((2 * DEPTH) ** 0.25)
LN_EPS = 1e-5
RMS_EPS = 1e-6

OFF_Q, OFF_F, OFF_I, OFF_G = 0, HGRN_WIDTH, 2 * HGRN_WIDTH, 3 * HGRN_WIDTH
OFF_B = 4 * HGRN_WIDTH
OFF_C = OFF_B + CONV_WIDTH
OFF_H = OFF_C + CONV_WIDTH

LOG2_E = math.log2(math.e)
DIAG = 128
LANE_PAD = 128
PROMPT_TILE = 256
SAMPLE_SEQS = 16
V7X_VMEM_LIMIT = 58 * 1024 * 1024

ROW_LN1G, ROW_LN1B, ROW_LN2G, ROW_LN2B, ROW_LOGLB, ROW_GATE, ROW_CW01, ROW_CW2 = range(8)


def _dot(a, b):
    return jnp.dot(a, b, preferred_element_type=F32)


def _dot_nt(a, b):
    return lax.dot_general(a, b, (((1,), (1,)), ((), ())), preferred_element_type=F32)


def _dot_tn(a, b):
    return lax.dot_general(a, b, (((0,), (0,)), ((), ())), preferred_element_type=F32)


def _sigmoid(x):
    return 1.0 / (1.0 + jnp.exp(-x))


def _layer_norm(y, g, b):
    mu = jnp.mean(y, axis=-1, keepdims=True)
    yc = y - mu
    var = jnp.mean(yc * yc, axis=-1, keepdims=True)
    return yc * lax.rsqrt(var + LN_EPS) * g + b


def _front(x, win_ref, vec_ref, tri_ref, proj_scr, q_scr, kk_scr, lf_scr, b_scr):
    proj_scr[...] = _dot(x.astype(BF16), win_ref[...])
    qp = proj_scr[:, OFF_Q:OFF_Q + HGRN_WIDTH]
    q_scr[...] = qp * _sigmoid(qp)
    z = proj_scr[:, OFF_F:OFF_F + HGRN_WIDTH]
    log_lb = vec_ref[ROW_LOGLB:ROW_LOGLB + 1, 0:HGRN_WIDTH]
    log_1mlb = vec_ref[ROW_LOGLB:ROW_LOGLB + 1, HGRN_WIDTH:2 * HGRN_WIDTH]
    one_m_lb = vec_ref[ROW_GATE:ROW_GATE + 1, 0:HGRN_WIDTH]
    e = jnp.exp(-jnp.abs(z))
    log_sig = jnp.minimum(z, 0.0) - jnp.log1p(e)
    c = log_1mlb + log_sig
    lf = LOG2_E * (jnp.maximum(log_lb, c) + jnp.log1p(jnp.exp(-jnp.abs(log_lb - c))))
    lf_scr[...] = lf
    kk_scr[...] = one_m_lb * jnp.where(z >= 0, e, 1.0) / (1.0 + e)
    hi = lf.astype(BF16)
    r1 = lf - hi.astype(F32)
    mid = r1.astype(BF16)
    lo = (r1 - mid.astype(F32)).astype(BF16)
    tri = tri_ref[...]
    b_scr[...] = _dot(tri, hi) + _dot(tri, mid) + _dot(tri, lo)


def _level_operand(h, half, tm, q_scr, kk_scr, lf_scr, b_scr):
    hs = slice(h * HEAD_DIM, (h + 1) * HEAD_DIM)
    if half >= 8:
        blk = 2 * half
        pieces = []
        for r0 in range(0, tm, blk):
            bm = b_scr[r0 + half - 1:r0 + half, hs]
            pieces.append(kk_scr[r0:r0 + half, hs] * jnp.exp2(bm - b_scr[r0:r0 + half, hs]))
            pieces.append(q_scr[r0 + half:r0 + blk, hs] * jnp.exp2(b_scr[r0 + half:r0 + blk, hs] - bm))
        return jnp.concatenate(pieces, axis=0).astype(BF16)
    row = lax.broadcasted_iota(jnp.int32, (tm, HEAD_DIM), 0)
    first = (row & half) == 0
    lfh = lf_scr[:, hs]
    if half == 4:
        bh = b_scr[:, hs]
        bm = jnp.concatenate([jnp.broadcast_to(b_scr[r0 + 3:r0 + 4, hs], (8, HEAD_DIM))
                              for r0 in range(0, tm, 8)], axis=0)
        arg = jnp.where(first, bm - bh, bh - bm)
    elif half == 2:
        r = row & 3
        lf_next = pltpu.roll(lfh, tm - 1, 0)
        lf_prev = pltpu.roll(lfh, 1, 0)
        arg = jnp.where(r == 0, lf_next, jnp.where(r == 1, 0.0, jnp.where(r == 2, lfh, lfh + lf_prev)))
    else:
        arg = jnp.where(first, 0.0, lfh)
    return (jnp.where(first, kk_scr[:, hs], q_scr[:, hs]) * jnp.exp2(arg)).astype(BF16)


def _intra_head(h, seg, tm, proj_scr, q_scr, kk_scr, lf_scr, b_scr, lv_ref):
    hs = slice(h * HEAD_DIM, (h + 1) * HEAD_DIM)
    vh = proj_scr[:, OFF_I + h * HEAD_DIM:OFF_I + (h + 1) * HEAD_DIM]
    n_diag = tm // DIAG
    lv = lv_ref[...]
    diag = [None] * n_diag
    cross = {}
    half = seg // 2
    while half >= 1:
        xl = _level_operand(h, half, tm, q_scr, kk_scr, lf_scr, b_scr)
        if half >= DIAG:
            for r0 in range(0, tm, 2 * half):
                for tq in range(r0 + half, r0 + 2 * half, DIAG):
                    for tk in range(r0, r0 + half, DIAG):
                        cross[(tq // DIAG, tk // DIAG)] = _dot_nt(xl[tq:tq + DIAG], xl[tk:tk + DIAG])
        else:
            lg = int(math.log2(half))
            for d in range(n_diag):
                xd = xl[d * DIAG:(d + 1) * DIAG]
                p = _dot_nt(xd, xd)
                diag[d] = jnp.where(lv == lg, p, 0.0 if diag[d] is None else diag[d])
        half //= 2
    vb = vh.astype(BF16)
    outs = []
    for d in range(n_diag):
        p = jnp.concatenate([cross[(d, e)] for e in range(d)] + [diag[d]], axis=1).astype(BF16)
        outs.append(_dot(p, vb[0:(d + 1) * DIAG]))
    dg = jnp.sum(q_scr[:, hs] * kk_scr[:, hs], axis=-1, keepdims=True)
    return jnp.concatenate(outs, axis=0) + dg * vh


def _finish_head(h, o, proj_scr, vec_ref, o_scr):
    hs = slice(h * HEAD_DIM, (h + 1) * HEAD_DIM)
    g = proj_scr[:, OFF_G + h * HEAD_DIM:OFF_G + (h + 1) * HEAD_DIM]
    onorm = vec_ref[ROW_GATE:ROW_GATE + 1, HGRN_WIDTH + h * HEAD_DIM:HGRN_WIDTH + (h + 1) * HEAD_DIM]
    o = o * lax.rsqrt(jnp.mean(o * o, axis=-1, keepdims=True) + RMS_EPS)
    o_scr[:, hs] = o * onorm * (g * _sigmoid(g))


def _mix_norm(x, mixin, wout_ref, vec_ref):
    mix = _dot(mixin, wout_ref[:, 0:D_MODEL])
    return _layer_norm(ALPHA * x + mix, vec_ref[ROW_LN1G:ROW_LN1G + 1, :], vec_ref[ROW_LN1B:ROW_LN1B + 1, :])


def _ffn_chunk(c, x1b, w1_ref, w2_ref):
    cs = slice(c * FF_CHUNK, (c + 1) * FF_CHUNK)
    hid = jnp.maximum(_dot(x1b, w1_ref[:, cs]), 0.0)
    return _dot((hid * hid).astype(BF16), w2_ref[cs, 0:D_MODEL])


def _ffn_norm(x1, ffn, vec_ref):
    return _layer_norm(ALPHA * x1 + ffn, vec_ref[ROW_LN2G:ROW_LN2G + 1, :], vec_ref[ROW_LN2B:ROW_LN2B + 1, :])


def _back(x, mixin, wout_ref, w1_ref, w2_ref, vec_ref):
    x1 = _mix_norm(x, mixin, wout_ref, vec_ref)
    x1b = x1.astype(BF16)
    acc = _ffn_chunk(0, x1b, w1_ref, w2_ref)
    for c in range(1, N_FF_CHUNKS):
        acc = acc + _ffn_chunk(c, x1b, w1_ref, w2_ref)
    return _ffn_norm(x1, acc, vec_ref)


def _conv_weights(vec_ref):
    w0 = vec_ref[ROW_CW01:ROW_CW01 + 1, 0:CONV_WIDTH]
    w1 = vec_ref[ROW_CW01:ROW_CW01 + 1, CONV_WIDTH:2 * CONV_WIDTH]
    w2 = vec_ref[ROW_CW2:ROW_CW2 + 1, 0:CONV_WIDTH]
    return w0, w1, w2


def _prompt_kernel(x_ref, win_ref, wout_ref, w1_ref, w2_ref, vec_ref, lv_ref, tri_ref,
                   y_ref, hst_ref, cst_ref,
                   proj_scr, q_scr, kk_scr, lf_scr, b_scr, u_scr, o_scr, xs_scr, mix_scr,
                   x1_scr, x1b_scr, acc_scr, *, n_tiles, tiles_per_seq):
    assert N_FF_CHUNKS == N_HEADS
    tm = PROMPT_TILE
    i = pl.program_id(0)
    live = i < n_tiles
    j = jnp.minimum(i, n_tiles - 1) % tiles_per_seq

    @pl.when(i == 0)
    def _():
        xs_scr[...] = jnp.zeros(xs_scr.shape, F32)
        mix_scr[...] = jnp.zeros(mix_scr.shape, BF16)

    @pl.when(jnp.logical_and(j == 0, live))
    def _():
        hst_ref[...] = jnp.zeros(hst_ref.shape, F32)
        u_scr[0:8, :] = jnp.zeros((8, CONV_WIDTH), F32)

    x1 = _mix_norm(xs_scr[...], mix_scr[...], wout_ref, vec_ref)
    x1_scr[...] = x1
    x1b_scr[...] = x1.astype(BF16)

    x = x_ref[0]
    _front(x, win_ref, vec_ref, tri_ref, proj_scr, q_scr, kk_scr, lf_scr, b_scr)

    for h in range(N_HEADS):
        ffn = _ffn_chunk(h, x1b_scr[...], w1_ref, w2_ref)
        acc_scr[...] = ffn if h == 0 else acc_scr[...] + ffn
        hs = slice(h * HEAD_DIM, (h + 1) * HEAD_DIM)
        o = _intra_head(h, tm, tm, proj_scr, q_scr, kk_scr, lf_scr, b_scr, lv_ref)
        qh = q_scr[:, hs]
        kh = kk_scr[:, hs]
        bh = b_scr[:, hs]
        vh = proj_scr[:, OFF_I + h * HEAD_DIM:OFF_I + (h + 1) * HEAD_DIM]
        s_prev = hst_ref[0, h]
        o = o + _dot((qh * jnp.exp2(bh)).astype(BF16), s_prev.astype(BF16))
        b_last = b_scr[tm - 1:tm, hs]
        kd = (kh * jnp.exp2(b_last - bh)).astype(BF16)
        upd = _dot_tn(kd, vh.astype(BF16))
        a_col = jnp.exp2(jnp.transpose(b_scr[tm - 8:tm, hs])[:, 7:8])
        hst_ref[0, h] = jnp.where(live, a_col * s_prev + upd, s_prev)
        _finish_head(h, o, proj_scr, vec_ref, o_scr)

    u = proj_scr[:, OFF_C:OFF_C + CONV_WIDTH] * proj_scr[:, OFF_H:OFF_H + CONV_WIDTH]
    u_scr[8:8 + tm, :] = u
    w0, w1, w2 = _conv_weights(vec_ref)
    conv = w0 * u_scr[6:6 + tm, :] + w1 * u_scr[7:7 + tm, :] + w2 * u
    yc = proj_scr[:, OFF_B:OFF_B + CONV_WIDTH] * conv
    tail = u_scr[tm:tm + 8, :]
    u_scr[0:8, :] = tail
    cst_ref[0] = tail[6:8, :]

    y_ref[0] = _ffn_norm(x1_scr[...], acc_scr[...], vec_ref)
    xs_scr[...] = x
    mix_scr[...] = jnp.concatenate([o_scr[...], yc], axis=-1).astype(BF16)


def _sample_kernel(x_ref, sin_ref, ext_ref, win_ref, wout_ref, w1_ref, w2_ref, vec_ref, lv_ref, tri_ref,
                   y_ref, sout_ref, u_ref,
                   proj_scr, q_scr, kk_scr, lf_scr, b_scr, o_scr, qe_scr, kd_scr, vb_scr, ys_scr):
    seg = 8
    tm = SAMPLE_SEQS * seg
    rows = pl.ds(pl.multiple_of(pl.program_id(1) * tm, tm), tm)

    @pl.when(pl.program_id(0) == 0)
    def _():
        ys_scr[rows, :] = x_ref[...]

    x = ys_scr[rows, :]
    _front(x, win_ref, vec_ref, tri_ref, proj_scr, q_scr, kk_scr, lf_scr, b_scr)

    for h in range(N_HEADS):
        hs = slice(h * HEAD_DIM, (h + 1) * HEAD_DIM)
        o_scr[:, hs] = _intra_head(h, seg, tm, proj_scr, q_scr, kk_scr, lf_scr, b_scr, lv_ref)
        bh = b_scr[:, hs]
        pieces = [jnp.broadcast_to(b_scr[pl.ds(seg * jb + seg - 1, 1), hs], (seg, HEAD_DIM))
                  for jb in range(tm // seg)]
        b_last = jnp.concatenate(pieces, axis=0)
        qe_scr[:, hs] = (q_scr[:, hs] * jnp.exp2(bh)).astype(BF16)
        kd_scr[:, hs] = (kk_scr[:, hs] * jnp.exp2(b_last - bh)).astype(BF16)
    vb_scr[...] = proj_scr[:, OFF_I:OFF_I + HGRN_WIDTH].astype(BF16)

    top = lax.broadcasted_iota(jnp.int32, (2 * seg, HEAD_DIM), 0) < seg

    def pair_body(p, carry):
        r0 = pl.multiple_of(p * 2 * seg, 2 * seg)
        for h in range(N_HEADS):
            hs = slice(h * HEAD_DIM, (h + 1) * HEAD_DIM)
            qe = qe_scr[pl.ds(r0, 2 * seg), hs]
            kd = kd_scr[pl.ds(r0, 2 * seg), hs]
            vv = vb_scr[pl.ds(r0, 2 * seg), hs]
            bt = jnp.transpose(b_scr[pl.ds(r0, 2 * seg), hs])
            s_a = sin_ref[2 * p, h]
            s_b = sin_ref[2 * p + 1, h]
            o_a = _dot(qe, s_a.astype(BF16))
            o_b = _dot(qe, s_b.astype(BF16))
            o_scr[pl.ds(r0, 2 * seg), hs] = o_scr[pl.ds(r0, 2 * seg), hs] + jnp.where(top, o_a, o_b)
            zero = jnp.zeros_like(kd)
            sout_ref[2 * p, h] = jnp.exp2(bt[:, seg - 1:seg]) * s_a + _dot_tn(jnp.where(top, kd, zero), vv)
            sout_ref[2 * p + 1, h] = jnp.exp2(bt[:, 2 * seg - 1:2 * seg]) * s_b + _dot_tn(jnp.where(top, zero, kd), vv)
        return carry

    lax.fori_loop(0, SAMPLE_SEQS // 2, pair_body, 0)

    for h in range(N_HEADS):
        hs = slice(h * HEAD_DIM, (h + 1) * HEAD_DIM)
        _finish_head(h, o_scr[:, hs], proj_scr, vec_ref, o_scr)

    u = proj_scr[:, OFF_C:OFF_C + CONV_WIDTH] * proj_scr[:, OFF_H:OFF_H + CONV_WIDTH]
    u_ref[...] = u
    ext = ext_ref[...]
    tmod = lax.broadcasted_iota(jnp.int32, (tm, CONV_WIDTH), 0) & (seg - 1)
    u1 = jnp.where(tmod == 0, pltpu.roll(ext, tm - 1, 0), pltpu.roll(u, 1, 0))
    u2 = jnp.where(tmod < 2, ext, pltpu.roll(u, 2, 0))
    w0, w1, w2 = _conv_weights(vec_ref)
    yc = proj_scr[:, OFF_B:OFF_B + CONV_WIDTH] * (w0 * u2 + w1 * u1 + w2 * u)

    mixin = jnp.concatenate([o_scr[...], yc], axis=-1).astype(BF16)
    y = _back(x, mixin, wout_ref, w1_ref, w2_ref, vec_ref)
    ys_scr[rows, :] = y
    y_ref[...] = y


def _level_table(tm, seg):
    t = np.arange(tm)[:, None]
    s = np.arange(tm)[None, :]
    x = np.maximum(t ^ s, 1)
    lv = np.floor(np.log2(x)).astype(np.int32)
    ok = (s < t) & ((t // seg) == (s // seg))
    return np.where(ok, lv, -1).astype(np.int32)


def _tri_table(tm, seg):
    t = np.arange(tm)[:, None]
    s = np.arange(tm)[None, :]
    return ((s <= t) & ((t // seg) == (s // seg))).astype(np.float32)


def _const_spec(shape):
    nd = len(shape)
    return pl.BlockSpec(shape, lambda *_: (0,) * nd, pipeline_mode=pl.Buffered(1))


def _weight_specs(layer_of):
    def wspec(k, n):
        return pl.BlockSpec((None, k, n), lambda *g: (layer_of(*g), 0, 0), pipeline_mode=pl.Buffered(1))
    return [wspec(D_MODEL, PROJ_WIDTH), wspec(D_MODEL, D_MODEL + LANE_PAD), wspec(D_MODEL, D_FF + LANE_PAD),
            wspec(D_FF, D_MODEL + LANE_PAD),
            wspec(8, D_MODEL)]


def _prompt_layer(layer, x, weights, vecs):
    bsz, seq, _ = x.shape
    tm = PROMPT_TILE
    tiles_per_seq = seq // tm
    n_tiles = bsz * tiles_per_seq
    lv = jnp.asarray(_level_table(DIAG, DIAG))
    tri = jnp.asarray(_tri_table(tm, tm), dtype=BF16)

    def cur(i):
        t = jnp.minimum(i, n_tiles - 1)
        return t // tiles_per_seq, t % tiles_per_seq

    def prev(i):
        t = jnp.maximum(i - 1, 0)
        return t // tiles_per_seq, t % tiles_per_seq

    return pl.pallas_call(
        functools.partial(_prompt_kernel, n_tiles=n_tiles, tiles_per_seq=tiles_per_seq),
        grid=(n_tiles + 1,),
        in_specs=[pl.BlockSpec((1, tm, D_MODEL), lambda i: (*cur(i), 0))] + _weight_specs(lambda i: layer)
        + [_const_spec((DIAG, DIAG)), _const_spec((tm, tm))],
        out_specs=[pl.BlockSpec((1, tm, D_MODEL), lambda i: (*prev(i), 0)),
                   pl.BlockSpec((1, N_HEADS, HEAD_DIM, HEAD_DIM), lambda i: (cur(i)[0], 0, 0, 0)),
                   pl.BlockSpec((1, CONV_K - 1, CONV_WIDTH), lambda i: (cur(i)[0], 0, 0))],
        out_shape=[jax.ShapeDtypeStruct((bsz, seq, D_MODEL), F32),
                   jax.ShapeDtypeStruct((bsz, N_HEADS, HEAD_DIM, HEAD_DIM), F32),
                   jax.ShapeDtypeStruct((bsz, CONV_K - 1, CONV_WIDTH), F32)],
        scratch_shapes=[pltpu.VMEM((tm, PROJ_WIDTH), F32),
                        pltpu.VMEM((tm, HGRN_WIDTH), F32), pltpu.VMEM((tm, HGRN_WIDTH), F32),
                        pltpu.VMEM((tm, HGRN_WIDTH), F32), pltpu.VMEM((tm, HGRN_WIDTH), F32),
                        pltpu.VMEM((tm + 8, CONV_WIDTH), F32), pltpu.VMEM((tm, HGRN_WIDTH), F32),
                        pltpu.VMEM((tm, D_MODEL), F32), pltpu.VMEM((tm, D_MODEL), BF16),
                        pltpu.VMEM((tm, D_MODEL), F32), pltpu.VMEM((tm, D_MODEL), BF16),
                        pltpu.VMEM((tm, D_MODEL), F32)],
        compiler_params=pltpu.CompilerParams(dimension_semantics=("arbitrary",),
                                             vmem_limit_bytes=V7X_VMEM_LIMIT),
        name=f"prompt_layer{layer}",
    )(x, *weights, vecs, lv, tri)


def _sample_layers(x, state, ext, weights, vecs):
    rows = x.shape[0]
    seg = 8
    tm = SAMPLE_SEQS * seg
    n_seq = rows // seg
    n_steps = n_seq // SAMPLE_SEQS
    lv = jnp.asarray(_level_table(DIAG, seg))
    tri = jnp.asarray(_tri_table(tm, seg), dtype=BF16)
    st_spec = pl.BlockSpec((None, SAMPLE_SEQS, N_HEADS, HEAD_DIM, HEAD_DIM), lambda l, i: (l, i, 0, 0, 0))
    u_spec = pl.BlockSpec((None, tm, CONV_WIDTH), lambda l, i: (l, i, 0))
    return pl.pallas_call(
        _sample_kernel,
        grid=(DEPTH, n_steps),
        in_specs=[pl.BlockSpec((tm, D_MODEL), lambda l, i: (i * (1 - l) + (n_steps - 1) * l, 0)), st_spec, u_spec]
        + _weight_specs(lambda l, i: l) + [_const_spec((DIAG, DIAG)), _const_spec((tm, tm))],
        out_specs=[pl.BlockSpec((tm, D_MODEL), lambda l, i: (i * l, 0)), st_spec, u_spec],
        out_shape=[jax.ShapeDtypeStruct((rows, D_MODEL), F32),
                   jax.ShapeDtypeStruct((DEPTH, n_seq, N_HEADS, HEAD_DIM, HEAD_DIM), F32),
                   jax.ShapeDtypeStruct((DEPTH, rows, CONV_WIDTH), F32)],
        scratch_shapes=[pltpu.VMEM((tm, PROJ_WIDTH), F32),
                        pltpu.VMEM((tm, HGRN_WIDTH), F32), pltpu.VMEM((tm, HGRN_WIDTH), F32),
                        pltpu.VMEM((tm, HGRN_WIDTH), F32), pltpu.VMEM((tm, HGRN_WIDTH), F32),
                        pltpu.VMEM((tm, HGRN_WIDTH), F32),
                        pltpu.VMEM((tm, HGRN_WIDTH), BF16), pltpu.VMEM((tm, HGRN_WIDTH), BF16),
                        pltpu.VMEM((tm, HGRN_WIDTH), BF16), pltpu.VMEM((rows, D_MODEL), F32)],
        compiler_params=pltpu.CompilerParams(dimension_semantics=("arbitrary", "arbitrary"),
                                             vmem_limit_bytes=V7X_VMEM_LIMIT),
        name="sample_layers",
    )(x, state, ext, *weights, vecs, lv, tri)


def kernel(x_prompt, x_sample, state_hgrn, state_conv, w_in, lb_logits, conv_w, onorm_g, w_out,
           ln1_g, ln1_b, w_ff1, w_ff2, ln2_g, ln2_b):
    n_seq, dec_seq, _ = x_sample.shape
    p = jax.nn.softmax(lb_logits.astype(F32), axis=0)
    cum = jnp.cumsum(p, axis=0)
    lb = cum - cum[0:1]
    zeros_half = jnp.zeros((DEPTH, CONV_WIDTH), F32)
    vecs = jnp.stack([
        ln1_g, ln1_b, ln2_g, ln2_b,
        jnp.concatenate([jnp.log(lb), jnp.log1p(-lb)], axis=-1),
        jnp.concatenate([1.0 - lb, onorm_g], axis=-1),
        jnp.concatenate([conv_w[:, 0], conv_w[:, 1]], axis=-1),
        jnp.concatenate([conv_w[:, 2], zeros_half], axis=-1)], axis=1).astype(F32)
    pad = ((0, 0), (0, 0), (0, LANE_PAD))
    weights = (w_in.astype(BF16), jnp.pad(w_out.astype(BF16), pad), jnp.pad(w_ff1.astype(BF16), pad),
               jnp.pad(w_ff2.astype(BF16), pad))
    ext = jnp.pad(state_conv, ((0, 0), (0, 0), (0, dec_seq - (CONV_K - 1)), (0, 0)))
    ext = ext.reshape(DEPTH, n_seq * dec_seq, CONV_WIDTH)

    yp = x_prompt
    ys = x_sample.reshape(n_seq * dec_seq, D_MODEL)
    hp, cp = [], []
    for layer in range(DEPTH):
        yp, hgrn_p, conv_p = _prompt_layer(layer, yp, weights, vecs)
        hp.append(hgrn_p)
        cp.append(conv_p)
    ys, hgrn_s, u_s = _sample_layers(ys, state_hgrn, ext, weights, vecs)
    conv_s = u_s.reshape(DEPTH, n_seq, dec_seq, CONV_WIDTH)[:, :, dec_seq - (CONV_K - 1):]
    return (yp, ys.reshape(n_seq, dec_seq, D_MODEL), jnp.stack(hp), jnp.stack(cp), hgrn_s, conv_s)
```

```python
import functools
import math

import numpy as np
import jax
import jax.numpy as jnp
from jax import lax
from jax.experimental import pallas as pl
from jax.experimental.pallas import tpu as pltpu

F32 = jnp.float32
BF16 = jnp.bfloat16

D_MODEL = 1024
DEPTH = 2
HGRN_WIDTH = 512
CONV_WIDTH = 512
HEAD_DIM = 128
N_HEADS = HGRN_WIDTH // HEAD_DIM
CONV_K = 3
D_FF = 4 * D_MODEL
PROJ_WIDTH = 4 * HGRN_WIDTH + 3 * CONV_WIDTH
FF_CHUNK = 1024
N_FF_CHUNKS = D_FF // FF_CHUNK
ALPHA = float((2 * DEPTH) ** 0.25)
LN_EPS = 1e-5
RMS_EPS = 1e-6

OFF_Q, OFF_F, OFF_I, OFF_G = 0, HGRN_WIDTH, 2 * HGRN_WIDTH, 3 * HGRN_WIDTH
OFF_B = 4 * HGRN_WIDTH
OFF_C = OFF_B + CONV_WIDTH
OFF_H = OFF_C + CONV_WIDTH

LOG2_E = math.log2(math.e)
DIAG = 128
FF_SUB = 256
PROMPT_TILE = 256
SAMPLE_SEQS = 16
V7X_VMEM_LIMIT = 58 * 1024 * 1024

ROW_LN1G, ROW_LN1B, ROW_LN2G, ROW_LN2B, ROW_LOGLB, ROW_GATE, ROW_CW01, ROW_CW2 = range(8)


def _dot(a, b):
    return jnp.dot(a, b, preferred_element_type=F32)


def _dot_nt(a, b):
    return lax.dot_general(a, b, (((1,), (1,)), ((), ())), preferred_element_type=F32)


def _dot_tn(a, b):
    return lax.dot_general(a, b, (((0,), (0,)), ((), ())), preferred_element_type=F32)


def _sigmoid(x):
    return 1.0 / (1.0 + jnp.exp(-x))


def _layer_norm(y, g, b):
    mu = jnp.mean(y, axis=-1, keepdims=True)
    yc = y - mu
    var = jnp.mean(yc * yc, axis=-1, keepdims=True)
    return yc * lax.rsqrt(var + LN_EPS) * g + b


def _front(x, win_ref, vec_ref, tri_ref, proj_scr, q_scr, kk_scr, lf_scr, b_scr):
    proj_scr[...] = _dot(x.astype(BF16), win_ref[...])
    qp = proj_scr[:, OFF_Q:OFF_Q + HGRN_WIDTH]
    q_scr[...] = qp * _sigmoid(qp)
    z = proj_scr[:, OFF_F:OFF_F + HGRN_WIDTH]
    log_lb = vec_ref[ROW_LOGLB:ROW_LOGLB + 1, 0:HGRN_WIDTH]
    log_1mlb = vec_ref[ROW_LOGLB:ROW_LOGLB + 1, HGRN_WIDTH:2 * HGRN_WIDTH]
    one_m_lb = vec_ref[ROW_GATE:ROW_GATE + 1, 0:HGRN_WIDTH]
    e = jnp.exp(-jnp.abs(z))
    log_sig = jnp.minimum(z, 0.0) - jnp.log1p(e)
    c = log_1mlb + log_sig
    lf = LOG2_E * (jnp.maximum(log_lb, c) + jnp.log1p(jnp.exp(-jnp.abs(log_lb - c))))
    lf_scr[...] = lf
    kk_scr[...] = one_m_lb * jnp.where(z >= 0, e, 1.0) / (1.0 + e)
    hi = lf.astype(BF16)
    r1 = lf - hi.astype(F32)
    mid = r1.astype(BF16)
    lo = (r1 - mid.astype(F32)).astype(BF16)
    tri = tri_ref[...]
    b_scr[...] = _dot(tri, hi) + _dot(tri, mid) + _dot(tri, lo)


def _level_operand(h, half, tm, q_scr, kk_scr, lf_scr, b_scr):
    hs = slice(h * HEAD_DIM, (h + 1) * HEAD_DIM)
    if half >= 8:
        blk = 2 * half
        pieces = []
        for r0 in range(0, tm, blk):
            bm = b_scr[r0 + half - 1:r0 + half, hs]
            pieces.append(kk_scr[r0:r0 + half, hs] * jnp.exp2(bm - b_scr[r0:r0 + half, hs]))
            pieces.append(q_scr[r0 + half:r0 + blk, hs] * jnp.exp2(b_scr[r0 + half:r0 + blk, hs] - bm))
        return jnp.concatenate(pieces, axis=0).astype(BF16)
    row = lax.broadcasted_iota(jnp.int32, (tm, HEAD_DIM), 0)
    first = (row & half) == 0
    lfh = lf_scr[:, hs]
    if half == 4:
        bh = b_scr[:, hs]
        bm = jnp.concatenate([jnp.broadcast_to(b_scr[r0 + 3:r0 + 4, hs], (8, HEAD_DIM))
                              for r0 in range(0, tm, 8)], axis=0)
        arg = jnp.where(first, bm - bh, bh - bm)
    elif half == 2:
        r = row & 3
        lf_next = pltpu.roll(lfh, tm - 1, 0)
        lf_prev = pltpu.roll(lfh, 1, 0)
        arg = jnp.where(r == 0, lf_next, jnp.where(r == 1, 0.0, jnp.where(r == 2, lfh, lfh + lf_prev)))
    else:
        arg = jnp.where(first, 0.0, lfh)
    return (jnp.where(first, kk_scr[:, hs], q_scr[:, hs]) * jnp.exp2(arg)).astype(BF16)


def _intra_head(h, seg, tm, proj_scr, q_scr, kk_scr, lf_scr, b_scr, lv_ref, fillers=()):
    fillers = list(fillers)
    hs = slice(h * HEAD_DIM, (h + 1) * HEAD_DIM)
    vh = proj_scr[:, OFF_I + h * HEAD_DIM:OFF_I + (h + 1) * HEAD_DIM]
    n_diag = tm // DIAG
    lv = lv_ref[...]
    diag = [None] * n_diag
    cross = {}
    half = seg // 2
    while half >= 1:
        xl = _level_operand(h, half, tm, q_scr, kk_scr, lf_scr, b_scr)
        if half >= DIAG:
            for r0 in range(0, tm, 2 * half):
                for tq in range(r0 + half, r0 + 2 * half, DIAG):
                    for tk in range(r0, r0 + half, DIAG):
                        cross[(tq // DIAG, tk // DIAG)] = _dot_nt(xl[tq:tq + DIAG], xl[tk:tk + DIAG])
        else:
            lg = int(math.log2(half))
            for d in range(n_diag):
                xd = xl[d * DIAG:(d + 1) * DIAG]
                p = _dot_nt(xd, xd)
                diag[d] = jnp.where(lv == lg, p, 0.0 if diag[d] is None else diag[d])
        if fillers:
            fillers.pop(0)()
        half //= 2
    for fill in fillers:
        fill()
    vb = vh.astype(BF16)
    outs = []
    for d in range(n_diag):
        p = jnp.concatenate([cross[(d, e)] for e in range(d)] + [diag[d]], axis=1).astype(BF16)
        outs.append(_dot(p, vb[0:(d + 1) * DIAG]))
    dg = jnp.sum(q_scr[:, hs] * kk_scr[:, hs], axis=-1, keepdims=True)
    return jnp.concatenate(outs, axis=0) + dg * vh


def _finish_head(h, o, proj_scr, vec_ref, o_scr):
    hs = slice(h * HEAD_DIM, (h + 1) * HEAD_DIM)
    g = proj_scr[:, OFF_G + h * HEAD_DIM:OFF_G + (h + 1) * HEAD_DIM]
    onorm = vec_ref[ROW_GATE:ROW_GATE + 1, HGRN_WIDTH + h * HEAD_DIM:HGRN_WIDTH + (h + 1) * HEAD_DIM]
    o = o * lax.rsqrt(jnp.mean(o * o, axis=-1, keepdims=True) + RMS_EPS)
    o_scr[:, hs] = o * onorm * (g * _sigmoid(g))


def _mix_norm(x, mixin, wout_ref, vec_ref):
    mix = _dot(mixin, wout_ref[...])
    return _layer_norm(ALPHA * x + mix, vec_ref[ROW_LN1G:ROW_LN1G + 1, :], vec_ref[ROW_LN1B:ROW_LN1B + 1, :])


def _ffn_chunk(c, x1b, w1_ref, w2_ref):
    cs = slice(c * FF_CHUNK, (c + 1) * FF_CHUNK)
    hid = jnp.maximum(_dot(x1b, w1_ref[:, cs]), 0.0)
    return _dot((hid * hid).astype(BF16), w2_ref[cs, :])


def _ffn_norm(x1, ffn, vec_ref):
    return _layer_norm(ALPHA * x1 + ffn, vec_ref[ROW_LN2G:ROW_LN2G + 1, :], vec_ref[ROW_LN2B:ROW_LN2B + 1, :])


def _back(x, mixin, wout_ref, w1_ref, w2_ref, vec_ref):
    x1 = _mix_norm(x, mixin, wout_ref, vec_ref)
    x1b = x1.astype(BF16)
    acc = _ffn_chunk(0, x1b, w1_ref, w2_ref)
    for c in range(1, N_FF_CHUNKS):
        acc = acc + _ffn_chunk(c, x1b, w1_ref, w2_ref)
    return _ffn_norm(x1, acc, vec_ref)


def _conv_weights(vec_ref):
    w0 = vec_ref[ROW_CW01:ROW_CW01 + 1, 0:CONV_WIDTH]
    w1 = vec_ref[ROW_CW01:ROW_CW01 + 1, CONV_WIDTH:2 * CONV_WIDTH]
    w2 = vec_ref[ROW_CW2:ROW_CW2 + 1, 0:CONV_WIDTH]
    return w0, w1, w2


def _prompt_kernel(x_ref, win_ref, wout_ref, w1_ref, w2_ref, vec_ref, lv_ref, tri_ref,
                   y_ref, hst_ref, cst_ref,
                   proj_scr, q_scr, kk_scr, lf_scr, b_scr, u_scr, o_scr, xs_scr, mix_scr,
                   x1_scr, x1b_scr, acc_scr, hid_scr, *, n_tiles, tiles_per_seq):
    assert N_FF_CHUNKS == N_HEADS
    tm = PROMPT_TILE
    i = pl.program_id(0)
    live = i < n_tiles
    j = jnp.minimum(i, n_tiles - 1) % tiles_per_seq

    @pl.when(i == 0)
    def _():
        xs_scr[...] = jnp.zeros(xs_scr.shape, F32)
        mix_scr[...] = jnp.zeros(mix_scr.shape, BF16)

    @pl.when(jnp.logical_and(j == 0, live))
    def _():
        hst_ref[...] = jnp.zeros(hst_ref.shape, F32)
        u_scr[0:8, :] = jnp.zeros((8, CONV_WIDTH), F32)

    x1 = _mix_norm(xs_scr[...], mix_scr[...], wout_ref, vec_ref)
    x1_scr[...] = x1
    x1b_scr[...] = x1.astype(BF16)

    x = x_ref[0]
    _front(x, win_ref, vec_ref, tri_ref, proj_scr, q_scr, kk_scr, lf_scr, b_scr)

    def ffn_fillers(c):
        def up(j):
            cols = slice(c * FF_CHUNK + j * FF_SUB, c * FF_CHUNK + (j + 1) * FF_SUB)
            hid = jnp.maximum(_dot(x1b_scr[...], w1_ref[:, cols]), 0.0)
            hid_scr[:, j * FF_SUB:(j + 1) * FF_SUB] = (hid * hid).astype(BF16)

        def down(j):
            cols = slice(j * FF_SUB, (j + 1) * FF_SUB)
            part = _dot(hid_scr[...], w2_ref[c * FF_CHUNK:(c + 1) * FF_CHUNK, cols])
            acc_scr[:, cols] = part if c == 0 else acc_scr[:, cols] + part

        return ([functools.partial(up, j) for j in range(FF_CHUNK // FF_SUB)]
                + [functools.partial(down, j) for j in range(D_MODEL // FF_SUB)])

    for h in range(N_HEADS):
        hs = slice(h * HEAD_DIM, (h + 1) * HEAD_DIM)
        o = _intra_head(h, tm, tm, proj_scr, q_scr, kk_scr, lf_scr, b_scr, lv_ref, ffn_fillers(h))
        qh = q_scr[:, hs]
        kh = kk_scr[:, hs]
        bh = b_scr[:, hs]
        vh = proj_scr[:, OFF_I + h * HEAD_DIM:OFF_I + (h + 1) * HEAD_DIM]
        s_prev = hst_ref[0, h]
        o = o + _dot((qh * jnp.exp2(bh)).astype(BF16), s_prev.astype(BF16))
        b_last = b_scr[tm - 1:tm, hs]
        kd = (kh * jnp.exp2(b_last - bh)).astype(BF16)
        upd = _dot_tn(kd, vh.astype(BF16))
        a_col = jnp.exp2(jnp.transpose(b_scr[tm - 8:tm, hs])[:, 7:8])
        hst_ref[0, h] = jnp.where(live, a_col * s_prev + upd, s_prev)
        _finish_head(h, o, proj_scr, vec_ref, o_scr)

    u = proj_scr[:, OFF_C:OFF_C + CONV_WIDTH] * proj_scr[:, OFF_H:OFF_H + CONV_WIDTH]
    u_scr[8:8 + tm, :] = u
    w0, w1, w2 = _conv_weights(vec_ref)
    conv = w0 * u_scr[6:6 + tm, :] + w1 * u_scr[7:7 + tm, :] + w2 * u
    yc = proj_scr[:, OFF_B:OFF_B + CONV_WIDTH] * conv
    tail = u_scr[tm:tm + 8, :]
    u_scr[0:8, :] = tail
    cst_ref[0] = tail[6:8, :]

    y_ref[0] = _ffn_norm(x1_scr[...], acc_scr[...], vec_ref)
    xs_scr[...] = x
    mix_scr[...] = jnp.concatenate([o_scr[...], yc], axis=-1).astype(BF16)


def _sample_kernel(x_ref, sin_ref, ext_ref, win_ref, wout_ref, w1_ref, w2_ref, vec_ref, lv_ref, tri_ref,
                   y_ref, sout_ref, u_ref,
                   proj_scr, q_scr, kk_scr, lf_scr, b_scr, o_scr, qe_scr, kd_scr, vb_scr, ys_scr):
    seg = 8
    tm = SAMPLE_SEQS * seg
    rows = pl.ds(pl.multiple_of(pl.program_id(1) * tm, tm), tm)

    @pl.when(pl.program_id(0) == 0)
    def _():
        ys_scr[rows, :] = x_ref[...]

    x = ys_scr[rows, :]
    _front(x, win_ref, vec_ref, tri_ref, proj_scr, q_scr, kk_scr, lf_scr, b_scr)

    for h in range(N_HEADS):
        hs = slice(h * HEAD_DIM, (h + 1) * HEAD_DIM)
        o_scr[:, hs] = _intra_head(h, seg, tm, proj_scr, q_scr, kk_scr, lf_scr, b_scr, lv_ref)
        bh = b_scr[:, hs]
        pieces = [jnp.broadcast_to(b_scr[pl.ds(seg * jb + seg - 1, 1), hs], (seg, HEAD_DIM))
                  for jb in range(tm // seg)]
        b_last = jnp.concatenate(pieces, axis=0)
        qe_scr[:, hs] = (q_scr[:, hs] * jnp.exp2(bh)).astype(BF16)
        kd_scr[:, hs] = (kk_scr[:, hs] * jnp.exp2(b_last - bh)).astype(BF16)
    vb_scr[...] = proj_scr[:, OFF_I:OFF_I + HGRN_WIDTH].astype(BF16)

    top = lax.broadcasted_iota(jnp.int32, (2 * seg, HEAD_DIM), 0) < seg

    def pair_body(p, carry):
        r0 = pl.multiple_of(p * 2 * seg, 2 * seg)
        for h in range(N_HEADS):
            hs = slice(h * HEAD_DIM, (h + 1) * HEAD_DIM)
            qe = qe_scr[pl.ds(r0, 2 * seg), hs]
            kd = kd_scr[pl.ds(r0, 2 * seg), hs]
            vv = vb_scr[pl.ds(r0, 2 * seg), hs]
            bt = jnp.transpose(b_scr[pl.ds(r0, 2 * seg), hs])
            s_a = sin_ref[2 * p, h]
            s_b = sin_ref[2 * p + 1, h]
            o_a = _dot(qe, s_a.astype(BF16))
            o_b = _dot(qe, s_b.astype(BF16))
            o_scr[pl.ds(r0, 2 * seg), hs] = o_scr[pl.ds(r0, 2 * seg), hs] + jnp.where(top, o_a, o_b)
            zero = jnp.zeros_like(kd)
            sout_ref[2 * p, h] = jnp.exp2(bt[:, seg - 1:seg]) * s_a + _dot_tn(jnp.where(top, kd, zero), vv)
            sout_ref[2 * p + 1, h] = jnp.exp2(bt[:, 2 * seg - 1:2 * seg]) * s_b + _dot_tn(jnp.where(top, zero, kd), vv)
        return carry

    lax.fori_loop(0, SAMPLE_SEQS // 2, pair_body, 0)

    for h in range(N_HEADS):
        hs = slice(h * HEAD_DIM, (h + 1) * HEAD_DIM)
        _finish_head(h, o_scr[:, hs], proj_scr, vec_ref, o_scr)

    u = proj_scr[:, OFF_C:OFF_C + CONV_WIDTH] * proj_scr[:, OFF_H:OFF_H + CONV_WIDTH]
    u_ref[...] = u
    ext = ext_ref[...]
    tmod = lax.broadcasted_iota(jnp.int32, (tm, CONV_WIDTH), 0) & (seg - 1)
    u1 = jnp.where(tmod == 0, pltpu.roll(ext, tm - 1, 0), pltpu.roll(u, 1, 0))
    u2 = jnp.where(tmod < 2, ext, pltpu.roll(u, 2, 0))
    w0, w1, w2 = _conv_weights(vec_ref)
    yc = proj_scr[:, OFF_B:OFF_B + CONV_WIDTH] * (w0 * u2 + w1 * u1 + w2 * u)

    mixin = jnp.concatenate([o_scr[...], yc], axis=-1).astype(BF16)
    y = _back(x, mixin, wout_ref, w1_ref, w2_ref, vec_ref)
    ys_scr[rows, :] = y
    y_ref[...] = y


def _level_table(tm, seg):
    t = np.arange(tm)[:, None]
    s = np.arange(tm)[None, :]
    x = np.maximum(t ^ s, 1)
    lv = np.floor(np.log2(x)).astype(np.int32)
    ok = (s < t) & ((t // seg) == (s // seg))
    return np.where(ok, lv, -1).astype(np.int32)


def _tri_table(tm, seg):
    t = np.arange(tm)[:, None]
    s = np.arange(tm)[None, :]
    return ((s <= t) & ((t // seg) == (s // seg))).astype(np.float32)


def _const_spec(shape):
    nd = len(shape)
    return pl.BlockSpec(shape, lambda *_: (0,) * nd, pipeline_mode=pl.Buffered(1))


def _weight_specs(layer_of):
    def wspec(k, n):
        return pl.BlockSpec((None, k, n), lambda *g: (layer_of(*g), 0, 0), pipeline_mode=pl.Buffered(1))
    return [wspec(D_MODEL, PROJ_WIDTH), wspec(D_MODEL, D_MODEL), wspec(D_MODEL, D_FF), wspec(D_FF, D_MODEL),
            wspec(8, D_MODEL)]


def _prompt_layer(layer, x, weights, vecs):
    bsz, seq, _ = x.shape
    tm = PROMPT_TILE
    tiles_per_seq = seq // tm
    n_tiles = bsz * tiles_per_seq
    lv = jnp.asarray(_level_table(DIAG, DIAG))
    tri = jnp.asarray(_tri_table(tm, tm), dtype=BF16)

    def cur(i):
        t = jnp.minimum(i, n_tiles - 1)
        return t // tiles_per_seq, t % tiles_per_seq

    def prev(i):
        t = jnp.maximum(i - 1, 0)
        return t // tiles_per_seq, t % tiles_per_seq

    return pl.pallas_call(
        functools.partial(_prompt_kernel, n_tiles=n_tiles, tiles_per_seq=tiles_per_seq),
        grid=(n_tiles + 1,),
        in_specs=[pl.BlockSpec((1, tm, D_MODEL), lambda i: (*cur(i), 0))] + _weight_specs(lambda i: layer)
        + [_const_spec((DIAG, DIAG)), _const_spec((tm, tm))],
        out_specs=[pl.BlockSpec((1, tm, D_MODEL), lambda i: (*prev(i), 0)),
                   pl.BlockSpec((1, N_HEADS, HEAD_DIM, HEAD_DIM), lambda i: (cur(i)[0], 0, 0, 0)),
                   pl.BlockSpec((1, CONV_K - 1, CONV_WIDTH), lambda i: (cur(i)[0], 0, 0))],
        out_shape=[jax.ShapeDtypeStruct((bsz, seq, D_MODEL), F32),
                   jax.ShapeDtypeStruct((bsz, N_HEADS, HEAD_DIM, HEAD_DIM), F32),
                   jax.ShapeDtypeStruct((bsz, CONV_K - 1, CONV_WIDTH), F32)],
        scratch_shapes=[pltpu.VMEM((tm, PROJ_WIDTH), F32),
                        pltpu.VMEM((tm, HGRN_WIDTH), F32), pltpu.VMEM((tm, HGRN_WIDTH), F32),
                        pltpu.VMEM((tm, HGRN_WIDTH), F32), pltpu.VMEM((tm, HGRN_WIDTH), F32),
                        pltpu.VMEM((tm + 8, CONV_WIDTH), F32), pltpu.VMEM((tm, HGRN_WIDTH), F32),
                        pltpu.VMEM((tm, D_MODEL), F32), pltpu.VMEM((tm, D_MODEL), BF16),
                        pltpu.VMEM((tm, D_MODEL), F32), pltpu.VMEM((tm, D_MODEL), BF16),
                        pltpu.VMEM((tm, D_MODEL), F32), pltpu.VMEM((tm, FF_CHUNK), BF16)],
        compiler_params=pltpu.CompilerParams(dimension_semantics=("arbitrary",),
                                             vmem_limit_bytes=V7X_VMEM_LIMIT),
        name=f"prompt_layer{layer}",
    )(x, *weights, vecs, lv, tri)


def _sample_layers(x, state, ext, weights, vecs):
    rows = x.shape[0]
    seg = 8
    tm = SAMPLE_SEQS * seg
    n_seq = rows // seg
    n_steps = n_seq // SAMPLE_SEQS
    lv = jnp.asarray(_level_table(DIAG, seg))
    tri = jnp.asarray(_tri_table(tm, seg), dtype=BF16)
    st_spec = pl.BlockSpec((None, SAMPLE_SEQS, N_HEADS, HEAD_DIM, HEAD_DIM), lambda l, i: (l, i, 0, 0, 0))
    u_spec = pl.BlockSpec((None, tm, CONV_WIDTH), lambda l, i: (l, i, 0))
    return pl.pallas_call(
        _sample_kernel,
        grid=(DEPTH, n_steps),
        in_specs=[pl.BlockSpec((tm, D_MODEL), lambda l, i: (i * (1 - l) + (n_steps - 1) * l, 0)), st_spec, u_spec]
        + _weight_specs(lambda l, i: l) + [_const_spec((DIAG, DIAG)), _const_spec((tm, tm))],
        out_specs=[pl.BlockSpec((tm, D_MODEL), lambda l, i: (i * l, 0)), st_spec, u_spec],
        out_shape=[jax.ShapeDtypeStruct((rows, D_MODEL), F32),
                   jax.ShapeDtypeStruct((DEPTH, n_seq, N_HEADS, HEAD_DIM, HEAD_DIM), F32),
                   jax.ShapeDtypeStruct((DEPTH, rows, CONV_WIDTH), F32)],
        scratch_shapes=[pltpu.VMEM((tm, PROJ_WIDTH), F32),
                        pltpu.VMEM((tm, HGRN_WIDTH), F32), pltpu.VMEM((tm, HGRN_WIDTH), F32),
                        pltpu.VMEM((tm, HGRN_WIDTH), F32), pltpu.VMEM((tm, HGRN_WIDTH), F32),
                        pltpu.VMEM((tm, HGRN_WIDTH), F32),
                        pltpu.VMEM((tm, HGRN_WIDTH), BF16), pltpu.VMEM((tm, HGRN_WIDTH), BF16),
                        pltpu.VMEM((tm, HGRN_WIDTH), BF16), pltpu.VMEM((rows, D_MODEL), F32)],
        compiler_params=pltpu.CompilerParams(dimension_semantics=("arbitrary", "arbitrary"),
                                             vmem_limit_bytes=V7X_VMEM_LIMIT),
        name="sample_layers",
    )(x, state, ext, *weights, vecs, lv, tri)


def kernel(x_prompt, x_sample, state_hgrn, state_conv, w_in, lb_logits, conv_w, onorm_g, w_out,
           ln1_g, ln1_b, w_ff1, w_ff2, ln2_g, ln2_b):
    n_seq, dec_seq, _ = x_sample.shape
    p = jax.nn.softmax(lb_logits.astype(F32), axis=0)
    cum = jnp.cumsum(p, axis=0)
    lb = cum - cum[0:1]
    zeros_half = jnp.zeros((DEPTH, CONV_WIDTH), F32)
    vecs = jnp.stack([
        ln1_g, ln1_b, ln2_g, ln2_b,
        jnp.concatenate([jnp.log(lb), jnp.log1p(-lb)], axis=-1),
        jnp.concatenate([1.0 - lb, onorm_g], axis=-1),
        jnp.concatenate([conv_w[:, 0], conv_w[:, 1]], axis=-1),
        jnp.concatenate([conv_w[:, 2], zeros_half], axis=-1)], axis=1).astype(F32)
    weights = (w_in.astype(BF16), w_out.astype(BF16), w_ff1.astype(BF16), w_ff2.astype(BF16))
    ext = jnp.pad(state_conv, ((0, 0), (0, 0), (0, dec_seq - (CONV_K - 1)), (0, 0)))
    ext = ext.reshape(DEPTH, n_seq * dec_seq, CONV_WIDTH)

    yp = x_prompt
    ys = x_sample.reshape(n_seq * dec_seq, D_MODEL)
    hp, cp = [], []
    for layer in range(DEPTH):
        yp, hgrn_p, conv_p = _prompt_layer(layer, yp, weights, vecs)
        hp.append(hgrn_p)
        cp.append(conv_p)
    ys, hgrn_s, u_s = _sample_layers(ys, state_hgrn, ext, weights, vecs)
    conv_s = u_s.reshape(DEPTH, n_seq, dec_seq, CONV_WIDTH)[:, :, dec_seq - (CONV_K - 1):]
    return (yp, ys.reshape(n_seq, dec_seq, D_MODEL), jnp.stack(hp), jnp.stack(cp), hgrn_s, conv_s)
```

```python
import functools
import math

import numpy as np
import jax
import jax.numpy as jnp
from jax import lax
from jax.experimental import pallas as pl
from jax.experimental.pallas import tpu as pltpu

F32 = jnp.float32
BF16 = jnp.bfloat16

D_MODEL = 1024
DEPTH = 2
HGRN_WIDTH = 512
CONV_WIDTH = 512
HEAD_DIM = 128
N_HEADS = HGRN_WIDTH // HEAD_DIM
CONV_K = 3
D_FF = 4 * D_MODEL
PROJ_WIDTH = 4 * HGRN_WIDTH + 3 * CONV_WIDTH
FF_CHUNK = 1024
N_FF_CHUNKS = D_FF // FF_CHUNK
ALPHA = float((2 * DEPTH) ** 0.25)
LN_EPS = 1e-5
RMS_EPS = 1e-6

OFF_Q, OFF_F, OFF_I, OFF_G = 0, HGRN_WIDTH, 2 * HGRN_WIDTH, 3 * HGRN_WIDTH
OFF_B = 4 * HGRN_WIDTH
OFF_C = OFF_B + CONV_WIDTH
OFF_H = OFF_C + CONV_WIDTH

LOG2_E = math.log2(math.e)
DIAG = 128
FF_SUB = 256
PROMPT_TILE = 256
SAMPLE_SEQS = 16
V7X_VMEM_LIMIT = 58 * 1024 * 1024

ROW_LN1G, ROW_LN1B, ROW_LN2G, ROW_LN2B, ROW_LOGLB, ROW_GATE, ROW_CW01, ROW_CW2 = range(8)


def _dot(a, b):
    return jnp.dot(a, b, preferred_element_type=F32)


def _dot_nt(a, b):
    return lax.dot_general(a, b, (((1,), (1,)), ((), ())), preferred_element_type=F32)


def _dot_tn(a, b):
    return lax.dot_general(a, b, (((0,), (0,)), ((), ())), preferred_element_type=F32)


def _sigmoid(x):
    return 1.0 / (1.0 + jnp.exp(-x))


def _layer_norm(y, g, b):
    mu = jnp.mean(y, axis=-1, keepdims=True)
    yc = y - mu
    var = jnp.mean(yc * yc, axis=-1, keepdims=True)
    return yc * lax.rsqrt(var + LN_EPS) * g + b


class _Fillers:
    def __init__(self, thunks):
        self._thunks = list(thunks)

    def issue(self):
        if self._thunks:
            self._thunks.pop(0)()

    def flush(self):
        while self._thunks:
            self._thunks.pop(0)()


def _front(x, win_ref, vec_ref, tri_ref, proj_scr, q_scr, kk_scr, lf_scr, b_scr, conv_fn=None):
    xb = x.astype(BF16)
    proj_scr[:, OFF_B:PROJ_WIDTH] = _dot(xb, win_ref[:, OFF_B:PROJ_WIDTH])
    proj_scr[:, OFF_Q:OFF_I] = _dot(xb, win_ref[:, OFF_Q:OFF_I])
    if conv_fn is not None:
        conv_fn()
    proj_scr[:, OFF_I:OFF_B] = _dot(xb, win_ref[:, OFF_I:OFF_B])
    qp = proj_scr[:, OFF_Q:OFF_Q + HGRN_WIDTH]
    q_scr[...] = qp * _sigmoid(qp)
    z = proj_scr[:, OFF_F:OFF_F + HGRN_WIDTH]
    log_lb = vec_ref[ROW_LOGLB:ROW_LOGLB + 1, 0:HGRN_WIDTH]
    log_1mlb = vec_ref[ROW_LOGLB:ROW_LOGLB + 1, HGRN_WIDTH:2 * HGRN_WIDTH]
    one_m_lb = vec_ref[ROW_GATE:ROW_GATE + 1, 0:HGRN_WIDTH]
    e = jnp.exp(-jnp.abs(z))
    log_sig = jnp.minimum(z, 0.0) - jnp.log1p(e)
    c = log_1mlb + log_sig
    lf = LOG2_E * (jnp.maximum(log_lb, c) + jnp.log1p(jnp.exp(-jnp.abs(log_lb - c))))
    lf_scr[...] = lf
    kk_scr[...] = one_m_lb * jnp.where(z >= 0, e, 1.0) / (1.0 + e)
    hi = lf.astype(BF16)
    r1 = lf - hi.astype(F32)
    mid = r1.astype(BF16)
    lo = (r1 - mid.astype(F32)).astype(BF16)
    tri = tri_ref[...]
    b_scr[...] = _dot(tri, hi) + _dot(tri, mid) + _dot(tri, lo)


def _level_operand(h, half, tm, q_scr, kk_scr, lf_scr, b_scr):
    hs = slice(h * HEAD_DIM, (h + 1) * HEAD_DIM)
    if half >= 8:
        blk = 2 * half
        pieces = []
        for r0 in range(0, tm, blk):
            bm = b_scr[r0 + half - 1:r0 + half, hs]
            pieces.append(kk_scr[r0:r0 + half, hs] * jnp.exp2(bm - b_scr[r0:r0 + half, hs]))
            pieces.append(q_scr[r0 + half:r0 + blk, hs] * jnp.exp2(b_scr[r0 + half:r0 + blk, hs] - bm))
        return jnp.concatenate(pieces, axis=0).astype(BF16)
    row = lax.broadcasted_iota(jnp.int32, (tm, HEAD_DIM), 0)
    first = (row & half) == 0
    lfh = lf_scr[:, hs]
    if half == 4:
        bh = b_scr[:, hs]
        bm = jnp.concatenate([jnp.broadcast_to(b_scr[r0 + 3:r0 + 4, hs], (8, HEAD_DIM))
                              for r0 in range(0, tm, 8)], axis=0)
        arg = jnp.where(first, bm - bh, bh - bm)
    elif half == 2:
        r = row & 3
        lf_next = pltpu.roll(lfh, tm - 1, 0)
        lf_prev = pltpu.roll(lfh, 1, 0)
        arg = jnp.where(r == 0, lf_next, jnp.where(r == 1, 0.0, jnp.where(r == 2, lfh, lfh + lf_prev)))
    else:
        arg = jnp.where(first, 0.0, lfh)
    return (jnp.where(first, kk_scr[:, hs], q_scr[:, hs]) * jnp.exp2(arg)).astype(BF16)


def _intra_head(h, seg, tm, proj_scr, q_scr, kk_scr, lf_scr, b_scr, lv_ref, fill=None):
    fill = fill or _Fillers(())
    hs = slice(h * HEAD_DIM, (h + 1) * HEAD_DIM)
    vh = proj_scr[:, OFF_I + h * HEAD_DIM:OFF_I + (h + 1) * HEAD_DIM]
    n_diag = tm // DIAG
    lv = lv_ref[...]
    diag = [None] * n_diag
    cross = {}
    half = seg // 2
    while half >= 1:
        xl = _level_operand(h, half, tm, q_scr, kk_scr, lf_scr, b_scr)
        if half >= DIAG:
            for r0 in range(0, tm, 2 * half):
                for tq in range(r0 + half, r0 + 2 * half, DIAG):
                    for tk in range(r0, r0 + half, DIAG):
                        cross[(tq // DIAG, tk // DIAG)] = _dot_nt(xl[tq:tq + DIAG], xl[tk:tk + DIAG])
        else:
            lg = int(math.log2(half))
            for d in range(n_diag):
                xd = xl[d * DIAG:(d + 1) * DIAG]
                p = _dot_nt(xd, xd)
                diag[d] = jnp.where(lv == lg, p, 0.0 if diag[d] is None else diag[d])
        fill.issue()
        half //= 2
    vb = vh.astype(BF16)
    outs = []
    for d in range(n_diag):
        p = jnp.concatenate([cross[(d, e)] for e in range(d)] + [diag[d]], axis=1).astype(BF16)
        outs.append(_dot(p, vb[0:(d + 1) * DIAG]))
    dg = jnp.sum(q_scr[:, hs] * kk_scr[:, hs], axis=-1, keepdims=True)
    return jnp.concatenate(outs, axis=0) + dg * vh


def _finish_head(h, o, proj_scr, vec_ref, o_scr):
    hs = slice(h * HEAD_DIM, (h + 1) * HEAD_DIM)
    g = proj_scr[:, OFF_G + h * HEAD_DIM:OFF_G + (h + 1) * HEAD_DIM]
    onorm = vec_ref[ROW_GATE:ROW_GATE + 1, HGRN_WIDTH + h * HEAD_DIM:HGRN_WIDTH + (h + 1) * HEAD_DIM]
    o = o * lax.rsqrt(jnp.mean(o * o, axis=-1, keepdims=True) + RMS_EPS)
    o_scr[:, hs] = o * onorm * (g * _sigmoid(g))


def _mix_norm(x, mixin, wout_ref, vec_ref):
    mix = _dot(mixin, wout_ref[...])
    return _layer_norm(ALPHA * x + mix, vec_ref[ROW_LN1G:ROW_LN1G + 1, :], vec_ref[ROW_LN1B:ROW_LN1B + 1, :])


def _ffn_chunk(c, x1b, w1_ref, w2_ref):
    cs = slice(c * FF_CHUNK, (c + 1) * FF_CHUNK)
    hid = jnp.maximum(_dot(x1b, w1_ref[:, cs]), 0.0)
    return _dot((hid * hid).astype(BF16), w2_ref[cs, :])


def _ffn_norm(x1, ffn, vec_ref):
    return _layer_norm(ALPHA * x1 + ffn, vec_ref[ROW_LN2G:ROW_LN2G + 1, :], vec_ref[ROW_LN2B:ROW_LN2B + 1, :])


def _back(x, mixin, wout_ref, w1_ref, w2_ref, vec_ref):
    x1 = _mix_norm(x, mixin, wout_ref, vec_ref)
    x1b = x1.astype(BF16)
    acc = _ffn_chunk(0, x1b, w1_ref, w2_ref)
    for c in range(1, N_FF_CHUNKS):
        acc = acc + _ffn_chunk(c, x1b, w1_ref, w2_ref)
    return _ffn_norm(x1, acc, vec_ref)


def _conv_weights(vec_ref):
    w0 = vec_ref[ROW_CW01:ROW_CW01 + 1, 0:CONV_WIDTH]
    w1 = vec_ref[ROW_CW01:ROW_CW01 + 1, CONV_WIDTH:2 * CONV_WIDTH]
    w2 = vec_ref[ROW_CW2:ROW_CW2 + 1, 0:CONV_WIDTH]
    return w0, w1, w2


def _prompt_kernel(x_ref, win_ref, wout_ref, w1_ref, w2_ref, vec_ref, lv_ref, tri_ref,
                   y_ref, hst_ref, cst_ref,
                   proj_scr, q_scr, kk_scr, lf_scr, b_scr, u_scr, o_scr, xs_scr, mix_scr,
                   x1_scr, x1b_scr, acc_scr, hid_scr, *, n_tiles, tiles_per_seq):
    tm = PROMPT_TILE
    i = pl.program_id(0)
    live = i < n_tiles
    j = jnp.minimum(i, n_tiles - 1) % tiles_per_seq

    @pl.when(i == 0)
    def _():
        xs_scr[...] = jnp.zeros(xs_scr.shape, F32)
        mix_scr[...] = jnp.zeros(mix_scr.shape, BF16)

    @pl.when(jnp.logical_and(j == 0, live))
    def _():
        hst_ref[...] = jnp.zeros(hst_ref.shape, F32)
        u_scr[0:8, :] = jnp.zeros((8, CONV_WIDTH), F32)

    x1 = _mix_norm(xs_scr[...], mix_scr[...], wout_ref, vec_ref)
    x1_scr[...] = x1
    x1b_scr[...] = x1.astype(BF16)

    x = x_ref[0]

    def conv_fn():
        u = proj_scr[:, OFF_C:OFF_C + CONV_WIDTH] * proj_scr[:, OFF_H:OFF_H + CONV_WIDTH]
        u_scr[8:8 + tm, :] = u
        w0, w1, w2 = _conv_weights(vec_ref)
        conv = w0 * u_scr[6:6 + tm, :] + w1 * u_scr[7:7 + tm, :] + w2 * u
        mix_scr[:, HGRN_WIDTH:HGRN_WIDTH + CONV_WIDTH] = (proj_scr[:, OFF_B:OFF_B + CONV_WIDTH] * conv).astype(BF16)
        tail = u_scr[tm:tm + 8, :]
        u_scr[0:8, :] = tail
        cst_ref[0] = tail[6:8, :]

    def ffn_fillers(c):
        def up(j):
            cols = slice(c * FF_CHUNK + j * FF_SUB, c * FF_CHUNK + (j + 1) * FF_SUB)
            hid = jnp.maximum(_dot(x1b_scr[...], w1_ref[:, cols]), 0.0)
            hid_scr[:, j * FF_SUB:(j + 1) * FF_SUB] = (hid * hid).astype(BF16)

        def down(j):
            cols = slice(j * FF_SUB, (j + 1) * FF_SUB)
            part = _dot(hid_scr[...], w2_ref[c * FF_CHUNK:(c + 1) * FF_CHUNK, cols])
            acc_scr[:, cols] = part if c == 0 else acc_scr[:, cols] + part

        return ([functools.partial(up, j) for j in range(FF_CHUNK // FF_SUB)]
                + [functools.partial(down, j) for j in range(D_MODEL // FF_SUB)])

    fill = _Fillers([f for c in range(N_FF_CHUNKS) for f in ffn_fillers(c)])
    _front(x, win_ref, vec_ref, tri_ref, proj_scr, q_scr, kk_scr, lf_scr, b_scr, conv_fn)

    for h in range(N_HEADS):
        hs = slice(h * HEAD_DIM, (h + 1) * HEAD_DIM)
        o = _intra_head(h, tm, tm, proj_scr, q_scr, kk_scr, lf_scr, b_scr, lv_ref, fill)
        qh = q_scr[:, hs]
        kh = kk_scr[:, hs]
        bh = b_scr[:, hs]
        vh = proj_scr[:, OFF_I + h * HEAD_DIM:OFF_I + (h + 1) * HEAD_DIM]
        s_prev = hst_ref[0, h]
        o = o + _dot((qh * jnp.exp2(bh)).astype(BF16), s_prev.astype(BF16))
        b_last = b_scr[tm - 1:tm, hs]
        kd = (kh * jnp.exp2(b_last - bh)).astype(BF16)
        upd = _dot_tn(kd, vh.astype(BF16))
        a_col = jnp.exp2(jnp.transpose(b_scr[tm - 8:tm, hs])[:, 7:8])
        hst_ref[0, h] = jnp.where(live, a_col * s_prev + upd, s_prev)
        _finish_head(h, o, proj_scr, vec_ref, o_scr)

    fill.flush()
    y_ref[0] = _ffn_norm(x1_scr[...], acc_scr[...], vec_ref)
    xs_scr[...] = x
    mix_scr[:, 0:HGRN_WIDTH] = o_scr[...].astype(BF16)


def _sample_kernel(x_ref, sin_ref, ext_ref, win_ref, wout_ref, w1_ref, w2_ref, vec_ref, lv_ref, tri_ref,
                   y_ref, sout_ref, u_ref,
                   proj_scr, q_scr, kk_scr, lf_scr, b_scr, o_scr, qe_scr, kd_scr, vb_scr, ys_scr):
    seg = 8
    tm = SAMPLE_SEQS * seg
    rows = pl.ds(pl.multiple_of(pl.program_id(1) * tm, tm), tm)

    @pl.when(pl.program_id(0) == 0)
    def _():
        ys_scr[rows, :] = x_ref[...]

    x = ys_scr[rows, :]
    _front(x, win_ref, vec_ref, tri_ref, proj_scr, q_scr, kk_scr, lf_scr, b_scr)

    for h in range(N_HEADS):
        hs = slice(h * HEAD_DIM, (h + 1) * HEAD_DIM)
        o_scr[:, hs] = _intra_head(h, seg, tm, proj_scr, q_scr, kk_scr, lf_scr, b_scr, lv_ref)
        bh = b_scr[:, hs]
        pieces = [jnp.broadcast_to(b_scr[pl.ds(seg * jb + seg - 1, 1), hs], (seg, HEAD_DIM))
                  for jb in range(tm // seg)]
        b_last = jnp.concatenate(pieces, axis=0)
        qe_scr[:, hs] = (q_scr[:, hs] * jnp.exp2(bh)).astype(BF16)
        kd_scr[:, hs] = (kk_scr[:, hs] * jnp.exp2(b_last - bh)).astype(BF16)
    vb_scr[...] = proj_scr[:, OFF_I:OFF_I + HGRN_WIDTH].astype(BF16)

    top = lax.broadcasted_iota(jnp.int32, (2 * seg, HEAD_DIM), 0) < seg
    for h in range(N_HEADS):
        hs = slice(h * HEAD_DIM, (h + 1) * HEAD_DIM)
        b_last = jnp.concatenate([b_scr[r:r + 1, hs] for r in range(seg - 1, tm, seg)], axis=0)
        a_cols = jnp.exp2(jnp.transpose(b_last))
        for p in range(SAMPLE_SEQS // 2):
            rows16 = slice(p * 2 * seg, (p + 1) * 2 * seg)
            qe = qe_scr[rows16, hs]
            kd = kd_scr[rows16, hs]
            vv = vb_scr[rows16, hs]
            s_a = sin_ref[2 * p, h]
            s_b = sin_ref[2 * p + 1, h]
            o_ab = _dot(qe, jnp.concatenate([s_a, s_b], axis=1).astype(BF16))
            o_scr[rows16, hs] = o_scr[rows16, hs] + jnp.where(top, o_ab[:, 0:HEAD_DIM], o_ab[:, HEAD_DIM:2 * HEAD_DIM])
            zero = jnp.zeros_like(kd)
            upd = _dot_tn(jnp.concatenate([jnp.where(top, kd, zero), jnp.where(top, zero, kd)], axis=1), vv)
            sout_ref[2 * p, h] = a_cols[:, 2 * p:2 * p + 1] * s_a + upd[0:HEAD_DIM]
            sout_ref[2 * p + 1, h] = a_cols[:, 2 * p + 1:2 * p + 2] * s_b + upd[HEAD_DIM:2 * HEAD_DIM]

    for h in range(N_HEADS):
        hs = slice(h * HEAD_DIM, (h + 1) * HEAD_DIM)
        _finish_head(h, o_scr[:, hs], proj_scr, vec_ref, o_scr)

    u = proj_scr[:, OFF_C:OFF_C + CONV_WIDTH] * proj_scr[:, OFF_H:OFF_H + CONV_WIDTH]
    u_ref[...] = u
    ext = ext_ref[...]
    tmod = lax.broadcasted_iota(jnp.int32, (tm, CONV_WIDTH), 0) & (seg - 1)
    u1 = jnp.where(tmod == 0, pltpu.roll(ext, tm - 1, 0), pltpu.roll(u, 1, 0))
    u2 = jnp.where(tmod < 2, ext, pltpu.roll(u, 2, 0))
    w0, w1, w2 = _conv_weights(vec_ref)
    yc = proj_scr[:, OFF_B:OFF_B + CONV_WIDTH] * (w0 * u2 + w1 * u1 + w2 * u)

    mixin = jnp.concatenate([o_scr[...], yc], axis=-1).astype(BF16)
    y = _back(x, mixin, wout_ref, w1_ref, w2_ref, vec_ref)
    ys_scr[rows, :] = y
    y_ref[...] = y


def _level_table(tm, seg):
    t = np.arange(tm)[:, None]
    s = np.arange(tm)[None, :]
    x = np.maximum(t ^ s, 1)
    lv = np.floor(np.log2(x)).astype(np.int32)
    ok = (s < t) & ((t // seg) == (s // seg))
    return np.where(ok, lv, -1).astype(np.int32)


def _tri_table(tm, seg):
    t = np.arange(tm)[:, None]
    s = np.arange(tm)[None, :]
    return ((s <= t) & ((t // seg) == (s // seg))).astype(np.float32)


def _const_spec(shape):
    nd = len(shape)
    return pl.BlockSpec(shape, lambda *_: (0,) * nd, pipeline_mode=pl.Buffered(1))


def _weight_specs(layer_of):
    def wspec(k, n):
        return pl.BlockSpec((None, k, n), lambda *g: (layer_of(*g), 0, 0), pipeline_mode=pl.Buffered(1))
    return [wspec(D_MODEL, PROJ_WIDTH), wspec(D_MODEL, D_MODEL), wspec(D_MODEL, D_FF), wspec(D_FF, D_MODEL),
            wspec(8, D_MODEL)]


def _prompt_layer(layer, x, weights, vecs):
    bsz, seq, _ = x.shape
    tm = PROMPT_TILE
    tiles_per_seq = seq // tm
    n_tiles = bsz * tiles_per_seq
    lv = jnp.asarray(_level_table(DIAG, DIAG))
    tri = jnp.asarray(_tri_table(tm, tm), dtype=BF16)

    def cur(i):
        t = jnp.minimum(i, n_tiles - 1)
        return t // tiles_per_seq, t % tiles_per_seq

    def prev(i):
        t = jnp.maximum(i - 1, 0)
        return t // tiles_per_seq, t % tiles_per_seq

    return pl.pallas_call(
        functools.partial(_prompt_kernel, n_tiles=n_tiles, tiles_per_seq=tiles_per_seq),
        grid=(n_tiles + 1,),
        in_specs=[pl.BlockSpec((1, tm, D_MODEL), lambda i: (*cur(i), 0))] + _weight_specs(lambda i: layer)
        + [_const_spec((DIAG, DIAG)), _const_spec((tm, tm))],
        out_specs=[pl.BlockSpec((1, tm, D_MODEL), lambda i: (*prev(i), 0)),
                   pl.BlockSpec((1, N_HEADS, HEAD_DIM, HEAD_DIM), lambda i: (cur(i)[0], 0, 0, 0)),
                   pl.BlockSpec((1, CONV_K - 1, CONV_WIDTH), lambda i: (cur(i)[0], 0, 0))],
        out_shape=[jax.ShapeDtypeStruct((bsz, seq, D_MODEL), F32),
                   jax.ShapeDtypeStruct((bsz, N_HEADS, HEAD_DIM, HEAD_DIM), F32),
                   jax.ShapeDtypeStruct((bsz, CONV_K - 1, CONV_WIDTH), F32)],
        scratch_shapes=[pltpu.VMEM((tm, PROJ_WIDTH), F32),
                        pltpu.VMEM((tm, HGRN_WIDTH), F32), pltpu.VMEM((tm, HGRN_WIDTH), F32),
                        pltpu.VMEM((tm, HGRN_WIDTH), F32), pltpu.VMEM((tm, HGRN_WIDTH), F32),
                        pltpu.VMEM((tm + 8, CONV_WIDTH), F32), pltpu.VMEM((tm, HGRN_WIDTH), F32),
                        pltpu.VMEM((tm, D_MODEL), F32), pltpu.VMEM((tm, D_MODEL), BF16),
                        pltpu.VMEM((tm, D_MODEL), F32), pltpu.VMEM((tm, D_MODEL), BF16),
                        pltpu.VMEM((tm, D_MODEL), F32), pltpu.VMEM((tm, FF_CHUNK), BF16)],
        compiler_params=pltpu.CompilerParams(dimension_semantics=("arbitrary",),
                                             vmem_limit_bytes=V7X_VMEM_LIMIT),
        name=f"prompt_layer{layer}",
    )(x, *weights, vecs, lv, tri)


def _sample_layers(x, state, ext, weights, vecs):
    rows = x.shape[0]
    seg = 8
    tm = SAMPLE_SEQS * seg
    n_seq = rows // seg
    n_steps = n_seq // SAMPLE_SEQS
    lv = jnp.asarray(_level_table(DIAG, seg))
    tri = jnp.asarray(_tri_table(tm, seg), dtype=BF16)
    st_spec = pl.BlockSpec((None, SAMPLE_SEQS, N_HEADS, HEAD_DIM, HEAD_DIM), lambda l, i: (l, i, 0, 0, 0))
    u_spec = pl.BlockSpec((None, tm, CONV_WIDTH), lambda l, i: (l, i, 0))
    return pl.pallas_call(
        _sample_kernel,
        grid=(DEPTH, n_steps),
        in_specs=[pl.BlockSpec((tm, D_MODEL), lambda l, i: (i * (1 - l) + (n_steps - 1) * l, 0)), st_spec, u_spec]
        + _weight_specs(lambda l, i: l) + [_const_spec((DIAG, DIAG)), _const_spec((tm, tm))],
        out_specs=[pl.BlockSpec((tm, D_MODEL), lambda l, i: (i * l, 0)), st_spec, u_spec],
        out_shape=[jax.ShapeDtypeStruct((rows, D_MODEL), F32),
                   jax.ShapeDtypeStruct((DEPTH, n_seq, N_HEADS, HEAD_DIM, HEAD_DIM), F32),
                   jax.ShapeDtypeStruct((DEPTH, rows, CONV_WIDTH), F32)],
        scratch_shapes=[pltpu.VMEM((tm, PROJ_WIDTH), F32),
                        pltpu.VMEM((tm, HGRN_WIDTH), F32), pltpu.VMEM((tm, HGRN_WIDTH), F32),
                        pltpu.VMEM((tm, HGRN_WIDTH), F32), pltpu.VMEM((tm, HGRN_WIDTH), F32),
                        pltpu.VMEM((tm, HGRN_WIDTH), F32),
                        pltpu.VMEM((tm, HGRN_WIDTH), BF16), pltpu.VMEM((tm, HGRN_WIDTH), BF16),
                        pltpu.VMEM((tm, HGRN_WIDTH), BF16), pltpu.VMEM((rows, D_MODEL), F32)],
        compiler_params=pltpu.CompilerParams(dimension_semantics=("arbitrary", "arbitrary"),
                                             vmem_limit_bytes=V7X_VMEM_LIMIT),
        name="sample_layers",
    )(x, state, ext, *weights, vecs, lv, tri)


def kernel(x_prompt, x_sample, state_hgrn, state_conv, w_in, lb_logits, conv_w, onorm_g, w_out,
           ln1_g, ln1_b, w_ff1, w_ff2, ln2_g, ln2_b):
    n_seq, dec_seq, _ = x_sample.shape
    p = jax.nn.softmax(lb_logits.astype(F32), axis=0)
    cum = jnp.cumsum(p, axis=0)
    lb = cum - cum[0:1]
    zeros_half = jnp.zeros((DEPTH, CONV_WIDTH), F32)
    vecs = jnp.stack([
        ln1_g, ln1_b, ln2_g, ln2_b,
        jnp.concatenate([jnp.log(lb), jnp.log1p(-lb)], axis=-1),
        jnp.concatenate([1.0 - lb, onorm_g], axis=-1),
        jnp.concatenate([conv_w[:, 0], conv_w[:, 1]], axis=-1),
        jnp.concatenate([conv_w[:, 2], zeros_half], axis=-1)], axis=1).astype(F32)
    weights = (w_in.astype(BF16), w_out.astype(BF16), w_ff1.astype(BF16), w_ff2.astype(BF16))
    ext = jnp.pad(state_conv, ((0, 0), (0, 0), (0, dec_seq - (CONV_K - 1)), (0, 0)))
    ext = ext.reshape(DEPTH, n_seq * dec_seq, CONV_WIDTH)

    yp = x_prompt
    ys = x_sample.reshape(n_seq * dec_seq, D_MODEL)
    hp, cp = [], []
    for layer in range(DEPTH):
        yp, hgrn_p, conv_p = _prompt_layer(layer, yp, weights, vecs)
        hp.append(hgrn_p)
        cp.append(conv_p)
    ys, hgrn_s, u_s = _sample_layers(ys, state_hgrn, ext, weights, vecs)
    conv_s = u_s.reshape(DEPTH, n_seq, dec_seq, CONV_WIDTH)[:, :, dec_seq - (CONV_K - 1):]
    return (yp, ys.reshape(n_seq, dec_seq, D_MODEL), jnp.stack(hp), jnp.stack(cp), hgrn_s, conv_s)
```

```python
import functools
import math

import numpy as np
import jax
import jax.numpy as jnp
from jax import lax
from jax.experimental import pallas as pl
from jax.experimental.pallas import tpu as pltpu

F32 = jnp.float32
BF16 = jnp.bfloat16

D_MODEL = 1024
DEPTH = 2
HGRN_WIDTH = 512
CONV_WIDTH = 512
HEAD_DIM = 128
N_HEADS = HGRN_WIDTH // HEAD_DIM
CONV_K = 3
D_FF = 4 * D_MODEL
PROJ_WIDTH = 4 * HGRN_WIDTH + 3 * CONV_WIDTH
FF_CHUNK = 1024
N_FF_CHUNKS = D_FF // FF_CHUNK
ALPHA = float((2 * DEPTH) ** 0.25)
LN_EPS = 1e-5
RMS_EPS = 1e-6

OFF_Q, OFF_F, OFF_I, OFF_G = 0, HGRN_WIDTH, 2 * HGRN_WIDTH, 3 * HGRN_WIDTH
OFF_B = 4 * HGRN_WIDTH
OFF_C = OFF_B + CONV_WIDTH
OFF_H = OFF_C + CONV_WIDTH

LOG2_E = math.log2(math.e)
DIAG = 128
FF_SUB = 256
PROMPT_TILE = 256
PROMPT_HEAD_GROUP = 1
SAMPLE_SEQS = 16
V7X_VMEM_LIMIT = 58 * 1024 * 1024

ROW_LN1G, ROW_LN1B, ROW_LN2G, ROW_LN2B, ROW_LOGLB, ROW_GATE, ROW_CW01, ROW_CW2 = range(8)


def _dot(a, b):
    return jnp.dot(a, b, preferred_element_type=F32)


def _dot_nt(a, b):
    return lax.dot_general(a, b, (((1,), (1,)), ((), ())), preferred_element_type=F32)


def _dot_tn(a, b):
    return lax.dot_general(a, b, (((0,), (0,)), ((), ())), preferred_element_type=F32)


def _sigmoid(x):
    return 1.0 / (1.0 + jnp.exp(-x))


def _layer_norm(y, g, b):
    mu = jnp.mean(y, axis=-1, keepdims=True)
    yc = y - mu
    var = jnp.mean(yc * yc, axis=-1, keepdims=True)
    return yc * lax.rsqrt(var + LN_EPS) * g + b


class _Fillers:
    def __init__(self, thunks):
        self._thunks = list(thunks)

    def issue(self):
        if self._thunks:
            self._thunks.pop(0)()

    def flush(self):
        while self._thunks:
            self._thunks.pop(0)()


def _front(x, win_ref, vec_ref, tri_ref, proj_scr, q_scr, kk_scr, lf_scr, b_scr, conv_fn=None):
    xb = x.astype(BF16)
    proj_scr[:, OFF_B:PROJ_WIDTH] = _dot(xb, win_ref[:, OFF_B:PROJ_WIDTH])
    proj_scr[:, OFF_Q:OFF_I] = _dot(xb, win_ref[:, OFF_Q:OFF_I])
    if conv_fn is not None:
        conv_fn()
    proj_scr[:, OFF_I:OFF_B] = _dot(xb, win_ref[:, OFF_I:OFF_B])
    qp = proj_scr[:, OFF_Q:OFF_Q + HGRN_WIDTH]
    q_scr[...] = qp * _sigmoid(qp)
    z = proj_scr[:, OFF_F:OFF_F + HGRN_WIDTH]
    log_lb = vec_ref[ROW_LOGLB:ROW_LOGLB + 1, 0:HGRN_WIDTH]
    log_1mlb = vec_ref[ROW_LOGLB:ROW_LOGLB + 1, HGRN_WIDTH:2 * HGRN_WIDTH]
    one_m_lb = vec_ref[ROW_GATE:ROW_GATE + 1, 0:HGRN_WIDTH]
    e = jnp.exp(-jnp.abs(z))
    log_sig = jnp.minimum(z, 0.0) - jnp.log1p(e)
    c = log_1mlb + log_sig
    lf = LOG2_E * (jnp.maximum(log_lb, c) + jnp.log1p(jnp.exp(-jnp.abs(log_lb - c))))
    lf_scr[...] = lf
    kk_scr[...] = one_m_lb * jnp.where(z >= 0, e, 1.0) / (1.0 + e)
    hi = lf.astype(BF16)
    r1 = lf - hi.astype(F32)
    mid = r1.astype(BF16)
    lo = (r1 - mid.astype(F32)).astype(BF16)
    tri = tri_ref[...]
    b_scr[...] = _dot(tri, hi) + _dot(tri, mid) + _dot(tri, lo)


def _level_operand(h, half, tm, q_scr, kk_scr, lf_scr, b_scr):
    hs = slice(h * HEAD_DIM, (h + 1) * HEAD_DIM)
    if half >= 8:
        blk = 2 * half
        pieces = []
        for r0 in range(0, tm, blk):
            bm = b_scr[r0 + half - 1:r0 + half, hs]
            pieces.append(kk_scr[r0:r0 + half, hs] * jnp.exp2(bm - b_scr[r0:r0 + half, hs]))
            pieces.append(q_scr[r0 + half:r0 + blk, hs] * jnp.exp2(b_scr[r0 + half:r0 + blk, hs] - bm))
        return jnp.concatenate(pieces, axis=0).astype(BF16)
    row = lax.broadcasted_iota(jnp.int32, (tm, HEAD_DIM), 0)
    first = (row & half) == 0
    lfh = lf_scr[:, hs]
    if half == 4:
        bh = b_scr[:, hs]
        bm = jnp.concatenate([jnp.broadcast_to(b_scr[r0 + 3:r0 + 4, hs], (8, HEAD_DIM))
                              for r0 in range(0, tm, 8)], axis=0)
        arg = jnp.where(first, bm - bh, bh - bm)
    elif half == 2:
        r = row & 3
        lf_next = pltpu.roll(lfh, tm - 1, 0)
        lf_prev = pltpu.roll(lfh, 1, 0)
        arg = jnp.where(r == 0, lf_next, jnp.where(r == 1, 0.0, jnp.where(r == 2, lfh, lfh + lf_prev)))
    else:
        arg = jnp.where(first, 0.0, lfh)
    return (jnp.where(first, kk_scr[:, hs], q_scr[:, hs]) * jnp.exp2(arg)).astype(BF16)


def _intra_heads(heads, seg, tm, proj_scr, q_scr, kk_scr, lf_scr, b_scr, lv_ref, fill=None):
    fill = fill or _Fillers(())
    heads = list(heads)
    n_diag = tm // DIAG
    lv = lv_ref[...]
    diag = {h: [None] * n_diag for h in heads}
    cross = {h: {} for h in heads}
    half = seg // 2
    while half >= 1:
        for h in heads:
            xl = _level_operand(h, half, tm, q_scr, kk_scr, lf_scr, b_scr)
            if half >= DIAG:
                for r0 in range(0, tm, 2 * half):
                    for tq in range(r0 + half, r0 + 2 * half, DIAG):
                        for tk in range(r0, r0 + half, DIAG):
                            cross[h][(tq // DIAG, tk // DIAG)] = _dot_nt(xl[tq:tq + DIAG], xl[tk:tk + DIAG])
            else:
                lg = int(math.log2(half))
                for d in range(n_diag):
                    xd = xl[d * DIAG:(d + 1) * DIAG]
                    prev = 0.0 if diag[h][d] is None else diag[h][d]
                    diag[h][d] = jnp.where(lv == lg, _dot_nt(xd, xd), prev)
            fill.issue()
        half //= 2
    outs = []
    for h in heads:
        hs = slice(h * HEAD_DIM, (h + 1) * HEAD_DIM)
        vh = proj_scr[:, OFF_I + h * HEAD_DIM:OFF_I + (h + 1) * HEAD_DIM]
        vb = vh.astype(BF16)
        rows = []
        for d in range(n_diag):
            p = jnp.concatenate([cross[h][(d, e)] for e in range(d)] + [diag[h][d]], axis=1).astype(BF16)
            rows.append(_dot(p, vb[0:(d + 1) * DIAG]))
        dg = jnp.sum(q_scr[:, hs] * kk_scr[:, hs], axis=-1, keepdims=True)
        outs.append(jnp.concatenate(rows, axis=0) + dg * vh)
    return outs


def _finish_head(h, o, proj_scr, vec_ref, o_scr):
    hs = slice(h * HEAD_DIM, (h + 1) * HEAD_DIM)
    g = proj_scr[:, OFF_G + h * HEAD_DIM:OFF_G + (h + 1) * HEAD_DIM]
    onorm = vec_ref[ROW_GATE:ROW_GATE + 1, HGRN_WIDTH + h * HEAD_DIM:HGRN_WIDTH + (h + 1) * HEAD_DIM]
    o = o * lax.rsqrt(jnp.mean(o * o, axis=-1, keepdims=True) + RMS_EPS)
    o_scr[:, hs] = o * onorm * (g * _sigmoid(g))


def _mix_norm(x, mixin, wout_ref, vec_ref):
    mix = _dot(mixin, wout_ref[...])
    return _layer_norm(ALPHA * x + mix, vec_ref[ROW_LN1G:ROW_LN1G + 1, :], vec_ref[ROW_LN1B:ROW_LN1B + 1, :])


def _ffn_chunk(c, x1b, w1_ref, w2_ref):
    cs = slice(c * FF_CHUNK, (c + 1) * FF_CHUNK)
    hid = jnp.maximum(_dot(x1b, w1_ref[:, cs]), 0.0)
    return _dot((hid * hid).astype(BF16), w2_ref[cs, :])


def _ffn_norm(x1, ffn, vec_ref):
    return _layer_norm(ALPHA * x1 + ffn, vec_ref[ROW_LN2G:ROW_LN2G + 1, :], vec_ref[ROW_LN2B:ROW_LN2B + 1, :])


def _back(x, mixin, wout_ref, w1_ref, w2_ref, vec_ref):
    x1 = _mix_norm(x, mixin, wout_ref, vec_ref)
    x1b = x1.astype(BF16)
    acc = _ffn_chunk(0, x1b, w1_ref, w2_ref)
    for c in range(1, N_FF_CHUNKS):
        acc = acc + _ffn_chunk(c, x1b, w1_ref, w2_ref)
    return _ffn_norm(x1, acc, vec_ref)


def _conv_weights(vec_ref):
    w0 = vec_ref[ROW_CW01:ROW_CW01 + 1, 0:CONV_WIDTH]
    w1 = vec_ref[ROW_CW01:ROW_CW01 + 1, CONV_WIDTH:2 * CONV_WIDTH]
    w2 = vec_ref[ROW_CW2:ROW_CW2 + 1, 0:CONV_WIDTH]
    return w0, w1, w2


def _prompt_kernel(x_ref, win_ref, wout_ref, w1_ref, w2_ref, vec_ref, lv_ref, tri_ref,
                   y_ref, hst_ref, cst_ref,
                   proj_scr, q_scr, kk_scr, lf_scr, b_scr, u_scr, o_scr, xs_scr, mix_scr,
                   x1_scr, x1b_scr, acc_scr, hid_scr, *, n_tiles, tiles_per_seq):
    tm = PROMPT_TILE
    i = pl.program_id(0)
    live = i < n_tiles
    j = jnp.minimum(i, n_tiles - 1) % tiles_per_seq

    @pl.when(i == 0)
    def _():
        xs_scr[...] = jnp.zeros(xs_scr.shape, F32)
        mix_scr[...] = jnp.zeros(mix_scr.shape, BF16)

    @pl.when(jnp.logical_and(j == 0, live))
    def _():
        hst_ref[...] = jnp.zeros(hst_ref.shape, F32)
        u_scr[0:8, :] = jnp.zeros((8, CONV_WIDTH), F32)

    x1 = _mix_norm(xs_scr[...], mix_scr[...], wout_ref, vec_ref)
    x1_scr[...] = x1
    x1b_scr[...] = x1.astype(BF16)

    x = x_ref[0]

    def conv_fn():
        u = proj_scr[:, OFF_C:OFF_C + CONV_WIDTH] * proj_scr[:, OFF_H:OFF_H + CONV_WIDTH]
        u_scr[8:8 + tm, :] = u
        w0, w1, w2 = _conv_weights(vec_ref)
        conv = w0 * u_scr[6:6 + tm, :] + w1 * u_scr[7:7 + tm, :] + w2 * u
        mix_scr[:, HGRN_WIDTH:HGRN_WIDTH + CONV_WIDTH] = (proj_scr[:, OFF_B:OFF_B + CONV_WIDTH] * conv).astype(BF16)
        tail = u_scr[tm:tm + 8, :]
        u_scr[0:8, :] = tail
        cst_ref[0] = tail[6:8, :]

    def ffn_fillers(c):
        def up(j):
            cols = slice(c * FF_CHUNK + j * FF_SUB, c * FF_CHUNK + (j + 1) * FF_SUB)
            hid = jnp.maximum(_dot(x1b_scr[...], w1_ref[:, cols]), 0.0)
            hid_scr[:, j * FF_SUB:(j + 1) * FF_SUB] = (hid * hid).astype(BF16)

        def down(j):
            cols = slice(j * FF_SUB, (j + 1) * FF_SUB)
            part = _dot(hid_scr[...], w2_ref[c * FF_CHUNK:(c + 1) * FF_CHUNK, cols])
            acc_scr[:, cols] = part if c == 0 else acc_scr[:, cols] + part

        return ([functools.partial(up, j) for j in range(FF_CHUNK // FF_SUB)]
                + [functools.partial(down, j) for j in range(D_MODEL // FF_SUB)])

    fill = _Fillers([f for c in range(N_FF_CHUNKS) for f in ffn_fillers(c)])
    _front(x, win_ref, vec_ref, tri_ref, proj_scr, q_scr, kk_scr, lf_scr, b_scr, conv_fn)

    for h in range(N_HEADS):
        if h % PROMPT_HEAD_GROUP == 0:
            group = range(h, h + PROMPT_HEAD_GROUP)
            intra = dict(zip(group, _intra_heads(group, tm, tm, proj_scr, q_scr, kk_scr, lf_scr, b_scr, lv_ref, fill)))
        hs = slice(h * HEAD_DIM, (h + 1) * HEAD_DIM)
        o = intra[h]
        qh = q_scr[:, hs]
        kh = kk_scr[:, hs]
        bh = b_scr[:, hs]
        vh = proj_scr[:, OFF_I + h * HEAD_DIM:OFF_I + (h + 1) * HEAD_DIM]
        s_prev = hst_ref[0, h]
        o = o + _dot((qh * jnp.exp2(bh)).astype(BF16), s_prev.astype(BF16))
        b_last = b_scr[tm - 1:tm, hs]
        kd = (kh * jnp.exp2(b_last - bh)).astype(BF16)
        upd = _dot_tn(kd, vh.astype(BF16))
        a_col = jnp.exp2(jnp.transpose(b_scr[tm - 8:tm, hs])[:, 7:8])
        hst_ref[0, h] = jnp.where(live, a_col * s_prev + upd, s_prev)
        _finish_head(h, o, proj_scr, vec_ref, o_scr)

    fill.flush()
    y_ref[0] = _ffn_norm(x1_scr[...], acc_scr[...], vec_ref)
    xs_scr[...] = x
    mix_scr[:, 0:HGRN_WIDTH] = o_scr[...].astype(BF16)


def _sample_kernel(x_ref, sin_ref, ext_ref, win_ref, wout_ref, w1_ref, w2_ref, vec_ref, lv_ref, tri_ref,
                   y_ref, sout_ref, u_ref,
                   proj_scr, q_scr, kk_scr, lf_scr, b_scr, o_scr, qe_scr, kd_scr, vb_scr, ys_scr):
    seg = 8
    tm = SAMPLE_SEQS * seg
    rows = pl.ds(pl.multiple_of(pl.program_id(1) * tm, tm), tm)

    @pl.when(pl.program_id(0) == 0)
    def _():
        ys_scr[rows, :] = x_ref[...]

    x = ys_scr[rows, :]
    _front(x, win_ref, vec_ref, tri_ref, proj_scr, q_scr, kk_scr, lf_scr, b_scr)

    intra = _intra_heads(range(N_HEADS), seg, tm, proj_scr, q_scr, kk_scr, lf_scr, b_scr, lv_ref)
    for h in range(N_HEADS):
        hs = slice(h * HEAD_DIM, (h + 1) * HEAD_DIM)
        o_scr[:, hs] = intra[h]
        bh = b_scr[:, hs]
        pieces = [jnp.broadcast_to(b_scr[pl.ds(seg * jb + seg - 1, 1), hs], (seg, HEAD_DIM))
                  for jb in range(tm // seg)]
        b_last = jnp.concatenate(pieces, axis=0)
        qe_scr[:, hs] = (q_scr[:, hs] * jnp.exp2(bh)).astype(BF16)
        kd_scr[:, hs] = (kk_scr[:, hs] * jnp.exp2(b_last - bh)).astype(BF16)
    vb_scr[...] = proj_scr[:, OFF_I:OFF_I + HGRN_WIDTH].astype(BF16)

    top = lax.broadcasted_iota(jnp.int32, (2 * seg, HEAD_DIM), 0) < seg
    for h in range(N_HEADS):
        hs = slice(h * HEAD_DIM, (h + 1) * HEAD_DIM)
        b_last = jnp.concatenate([b_scr[r:r + 1, hs] for r in range(seg - 1, tm, seg)], axis=0)
        a_cols = jnp.exp2(jnp.transpose(b_last))
        for p in range(SAMPLE_SEQS // 2):
            rows16 = slice(p * 2 * seg, (p + 1) * 2 * seg)
            qe = qe_scr[rows16, hs]
            kd = kd_scr[rows16, hs]
            vv = vb_scr[rows16, hs]
            s_a = sin_ref[2 * p, h]
            s_b = sin_ref[2 * p + 1, h]
            o_ab = _dot(qe, jnp.concatenate([s_a, s_b], axis=1).astype(BF16))
            o_scr[rows16, hs] = o_scr[rows16, hs] + jnp.where(top, o_ab[:, 0:HEAD_DIM], o_ab[:, HEAD_DIM:2 * HEAD_DIM])
            zero = jnp.zeros_like(kd)
            upd = _dot_tn(jnp.concatenate([jnp.where(top, kd, zero), jnp.where(top, zero, kd)], axis=1), vv)
            sout_ref[2 * p, h] = a_cols[:, 2 * p:2 * p + 1] * s_a + upd[0:HEAD_DIM]
            sout_ref[2 * p + 1, h] = a_cols[:, 2 * p + 1:2 * p + 2] * s_b + upd[HEAD_DIM:2 * HEAD_DIM]

    for h in range(N_HEADS):
        hs = slice(h * HEAD_DIM, (h + 1) * HEAD_DIM)
        _finish_head(h, o_scr[:, hs], proj_scr, vec_ref, o_scr)

    u = proj_scr[:, OFF_C:OFF_C + CONV_WIDTH] * proj_scr[:, OFF_H:OFF_H + CONV_WIDTH]
    u_ref[...] = u
    ext = ext_ref[...]
    tmod = lax.broadcasted_iota(jnp.int32, (tm, CONV_WIDTH), 0) & (seg - 1)
    u1 = jnp.where(tmod == 0, pltpu.roll(ext, tm - 1, 0), pltpu.roll(u, 1, 0))
    u2 = jnp.where(tmod < 2, ext, pltpu.roll(u, 2, 0))
    w0, w1, w2 = _conv_weights(vec_ref)
    yc = proj_scr[:, OFF_B:OFF_B + CONV_WIDTH] * (w0 * u2 + w1 * u1 + w2 * u)

    mixin = jnp.concatenate([o_scr[...], yc], axis=-1).astype(BF16)
    y = _back(x, mixin, wout_ref, w1_ref, w2_ref, vec_ref)
    ys_scr[rows, :] = y
    y_ref[...] = y


def _level_table(tm, seg):
    t = np.arange(tm)[:, None]
    s = np.arange(tm)[None, :]
    x = np.maximum(t ^ s, 1)
    lv = np.floor(np.log2(x)).astype(np.int32)
    ok = (s < t) & ((t // seg) == (s // seg))
    return np.where(ok, lv, -1).astype(np.int32)


def _tri_table(tm, seg):
    t = np.arange(tm)[:, None]
    s = np.arange(tm)[None, :]
    return ((s <= t) & ((t // seg) == (s // seg))).astype(np.float32)


def _const_spec(shape):
    nd = len(shape)
    return pl.BlockSpec(shape, lambda *_: (0,) * nd, pipeline_mode=pl.Buffered(1))


def _weight_specs(layer_of):
    def wspec(k, n):
        return pl.BlockSpec((None, k, n), lambda *g: (layer_of(*g), 0, 0), pipeline_mode=pl.Buffered(1))
    return [wspec(D_MODEL, PROJ_WIDTH), wspec(D_MODEL, D_MODEL), wspec(D_MODEL, D_FF), wspec(D_FF, D_MODEL),
            wspec(8, D_MODEL)]


def _prompt_layer(layer, x, weights, vecs):
    bsz, seq, _ = x.shape
    tm = PROMPT_TILE
    tiles_per_seq = seq // tm
    n_tiles = bsz * tiles_per_seq
    lv = jnp.asarray(_level_table(DIAG, DIAG))
    tri = jnp.asarray(_tri_table(tm, tm), dtype=BF16)

    def cur(i):
        t = jnp.minimum(i, n_tiles - 1)
        return t // tiles_per_seq, t % tiles_per_seq

    def prev(i):
        t = jnp.maximum(i - 1, 0)
        return t // tiles_per_seq, t % tiles_per_seq

    return pl.pallas_call(
        functools.partial(_prompt_kernel, n_tiles=n_tiles, tiles_per_seq=tiles_per_seq),
        grid=(n_tiles + 1,),
        in_specs=[pl.BlockSpec((1, tm, D_MODEL), lambda i: (*cur(i), 0))] + _weight_specs(lambda i: layer)
        + [_const_spec((DIAG, DIAG)), _const_spec((tm, tm))],
        out_specs=[pl.BlockSpec((1, tm, D_MODEL), lambda i: (*prev(i), 0)),
                   pl.BlockSpec((1, N_HEADS, HEAD_DIM, HEAD_DIM), lambda i: (cur(i)[0], 0, 0, 0)),
                   pl.BlockSpec((1, CONV_K - 1, CONV_WIDTH), lambda i: (cur(i)[0], 0, 0))],
        out_shape=[jax.ShapeDtypeStruct((bsz, seq, D_MODEL), F32),
                   jax.ShapeDtypeStruct((bsz, N_HEADS, HEAD_DIM, HEAD_DIM), F32),
                   jax.ShapeDtypeStruct((bsz, CONV_K - 1, CONV_WIDTH), F32)],
        scratch_shapes=[pltpu.VMEM((tm, PROJ_WIDTH), F32),
                        pltpu.VMEM((tm, HGRN_WIDTH), F32), pltpu.VMEM((tm, HGRN_WIDTH), F32),
                        pltpu.VMEM((tm, HGRN_WIDTH), F32), pltpu.VMEM((tm, HGRN_WIDTH), F32),
                        pltpu.VMEM((tm + 8, CONV_WIDTH), F32), pltpu.VMEM((tm, HGRN_WIDTH), F32),
                        pltpu.VMEM((tm, D_MODEL), F32), pltpu.VMEM((tm, D_MODEL), BF16),
                        pltpu.VMEM((tm, D_MODEL), F32), pltpu.VMEM((tm, D_MODEL), BF16),
                        pltpu.VMEM((tm, D_MODEL), F32), pltpu.VMEM((tm, FF_CHUNK), BF16)],
        compiler_params=pltpu.CompilerParams(dimension_semantics=("arbitrary",),
                                             vmem_limit_bytes=V7X_VMEM_LIMIT),
        name=f"prompt_layer{layer}",
    )(x, *weights, vecs, lv, tri)


def _sample_layers(x, state, ext, weights, vecs):
    rows = x.shape[0]
    seg = 8
    tm = SAMPLE_SEQS * seg
    n_seq = rows // seg
    n_steps = n_seq // SAMPLE_SEQS
    lv = jnp.asarray(_level_table(DIAG, seg))
    tri = jnp.asarray(_tri_table(tm, seg), dtype=BF16)
    st_spec = pl.BlockSpec((None, SAMPLE_SEQS, N_HEADS, HEAD_DIM, HEAD_DIM), lambda l, i: (l, i, 0, 0, 0))
    u_spec = pl.BlockSpec((None, tm, CONV_WIDTH), lambda l, i: (l, i, 0))
    return pl.pallas_call(
        _sample_kernel,
        grid=(DEPTH, n_steps),
        in_specs=[pl.BlockSpec((tm, D_MODEL), lambda l, i: (i * (1 - l) + (n_steps - 1) * l, 0)), st_spec, u_spec]
        + _weight_specs(lambda l, i: l) + [_const_spec((DIAG, DIAG)), _const_spec((tm, tm))],
        out_specs=[pl.BlockSpec((tm, D_MODEL), lambda l, i: (i * l, 0)), st_spec, u_spec],
        out_shape=[jax.ShapeDtypeStruct((rows, D_MODEL), F32),
                   jax.ShapeDtypeStruct((DEPTH, n_seq, N_HEADS, HEAD_DIM, HEAD_DIM), F32),
                   jax.ShapeDtypeStruct((DEPTH, rows, CONV_WIDTH), F32)],
        scratch_shapes=[pltpu.VMEM((tm, PROJ_WIDTH), F32),
                        pltpu.VMEM((tm, HGRN_WIDTH), F32), pltpu.VMEM((tm, HGRN_WIDTH), F32),
                        pltpu.VMEM((tm, HGRN_WIDTH), F32), pltpu.VMEM((tm, HGRN_WIDTH), F32),
                        pltpu.VMEM((tm, HGRN_WIDTH), F32),
                        pltpu.VMEM((tm, HGRN_WIDTH), BF16), pltpu.VMEM((tm, HGRN_WIDTH), BF16),
                        pltpu.VMEM((tm, HGRN_WIDTH), BF16), pltpu.VMEM((rows, D_MODEL), F32)],
        compiler_params=pltpu.CompilerParams(dimension_semantics=("arbitrary", "arbitrary"),
                                             vmem_limit_bytes=V7X_VMEM_LIMIT),
        name="sample_layers",
    )(x, state, ext, *weights, vecs, lv, tri)


def kernel(x_prompt, x_sample, state_hgrn, state_conv, w_in, lb_logits, conv_w, onorm_g, w_out,
           ln1_g, ln1_b, w_ff1, w_ff2, ln2_g, ln2_b):
    n_seq, dec_seq, _ = x_sample.shape
    p = jax.nn.softmax(lb_logits.astype(F32), axis=0)
    cum = jnp.cumsum(p, axis=0)
    lb = cum - cum[0:1]
    zeros_half = jnp.zeros((DEPTH, CONV_WIDTH), F32)
    vecs = jnp.stack([
        ln1_g, ln1_b, ln2_g, ln2_b,
        jnp.concatenate([jnp.log(lb), jnp.log1p(-lb)], axis=-1),
        jnp.concatenate([1.0 - lb, onorm_g], axis=-1),
        jnp.concatenate([conv_w[:, 0], conv_w[:, 1]], axis=-1),
        jnp.concatenate([conv_w[:, 2], zeros_half], axis=-1)], axis=1).astype(F32)
    weights = (w_in.astype(BF16), w_out.astype(BF16), w_ff1.astype(BF16), w_ff2.astype(BF16))
    ext = jnp.pad(state_conv, ((0, 0), (0, 0), (0, dec_seq - (CONV_K - 1)), (0, 0)))
    ext = ext.reshape(DEPTH, n_seq * dec_seq, CONV_WIDTH)

    yp = x_prompt
    ys = x_sample.reshape(n_seq * dec_seq, D_MODEL)
    hp, cp = [], []
    for layer in range(DEPTH):
        yp, hgrn_p, conv_p = _prompt_layer(layer, yp, weights, vecs)
        hp.append(hgrn_p)
        cp.append(conv_p)
    ys, hgrn_s, u_s = _sample_layers(ys, state_hgrn, ext, weights, vecs)
    conv_s = u_s.reshape(DEPTH, n_seq, dec_seq, CONV_WIDTH)[:, :, dec_seq - (CONV_K - 1):]
    return (yp, ys.reshape(n_seq, dec_seq, D_MODEL), jnp.stack(hp), jnp.stack(cp), hgrn_s, conv_s)
```

```python
import functools
import math

import numpy as np
import jax
import jax.numpy as jnp
from jax import lax
from jax.experimental import pallas as pl
from jax.experimental.pallas import tpu as pltpu

F32 = jnp.float32
BF16 = jnp.bfloat16

D_MODEL = 1024
DEPTH = 2
HGRN_WIDTH = 512
CONV_WIDTH = 512
HEAD_DIM = 128
N_HEADS = HGRN_WIDTH // HEAD_DIM
CONV_K = 3
D_FF = 4 * D_MODEL
PROJ_WIDTH = 4 * HGRN_WIDTH + 3 * CONV_WIDTH
FF_CHUNK = 1024
N_FF_CHUNKS = D_FF // FF_CHUNK
ALPHA = float((2 * DEPTH) ** 0.25)
LN_EPS = 1e-5
RMS_EPS = 1e-6

OFF_Q, OFF_F, OFF_I, OFF_G = 0, HGRN_WIDTH, 2 * HGRN_WIDTH, 3 * HGRN_WIDTH
OFF_B = 4 * HGRN_WIDTH
OFF_C = OFF_B + CONV_WIDTH
OFF_H = OFF_C + CONV_WIDTH

LOG2_E = math.log2(math.e)
DIAG = 128
SCORE_GROUP = 2
FF_SUB = 256
PROMPT_TILE = 256
SAMPLE_SEQS = 16
V7X_VMEM_LIMIT = 58 * 1024 * 1024

ROW_LN1G, ROW_LN1B, ROW_LN2G, ROW_LN2B, ROW_LOGLB, ROW_GATE, ROW_CW01, ROW_CW2 = range(8)


def _dot(a, b):
    return jnp.dot(a, b, preferred_element_type=F32)


def _dot_nt(a, b):
    return lax.dot_general(a, b, (((1,), (1,)), ((), ())), preferred_element_type=F32)


def _dot_tn(a, b):
    return lax.dot_general(a, b, (((0,), (0,)), ((), ())), preferred_element_type=F32)


def _sigmoid(x):
    return 1.0 / (1.0 + jnp.exp(-x))


def _layer_norm(y, g, b):
    mu = jnp.mean(y, axis=-1, keepdims=True)
    yc = y - mu
    var = jnp.mean(yc * yc, axis=-1, keepdims=True)
    return yc * lax.rsqrt(var + LN_EPS) * g + b


class _Fillers:
    def __init__(self, thunks):
        self._thunks = list(thunks)

    def issue(self):
        if self._thunks:
            self._thunks.pop(0)()

    def flush(self):
        while self._thunks:
            self._thunks.pop(0)()


def _front(x, win_ref, vec_ref, tri_ref, proj_scr, q_scr, kk_scr, lf_scr, b_scr, conv_fn=None):
    xb = x.astype(BF16)
    proj_scr[:, OFF_B:PROJ_WIDTH] = _dot(xb, win_ref[:, OFF_B:PROJ_WIDTH])
    proj_scr[:, OFF_Q:OFF_I] = _dot(xb, win_ref[:, OFF_Q:OFF_I])
    if conv_fn is not None:
        conv_fn()
    proj_scr[:, OFF_I:OFF_B] = _dot(xb, win_ref[:, OFF_I:OFF_B])
    qp = proj_scr[:, OFF_Q:OFF_Q + HGRN_WIDTH]
    q_scr[...] = qp * _sigmoid(qp)
    z = proj_scr[:, OFF_F:OFF_F + HGRN_WIDTH]
    log_lb = vec_ref[ROW_LOGLB:ROW_LOGLB + 1, 0:HGRN_WIDTH]
    log_1mlb = vec_ref[ROW_LOGLB:ROW_LOGLB + 1, HGRN_WIDTH:2 * HGRN_WIDTH]
    one_m_lb = vec_ref[ROW_GATE:ROW_GATE + 1, 0:HGRN_WIDTH]
    e = jnp.exp(-jnp.abs(z))
    log_sig = jnp.minimum(z, 0.0) - jnp.log1p(e)
    c = log_1mlb + log_sig
    lf = LOG2_E * (jnp.maximum(log_lb, c) + jnp.log1p(jnp.exp(-jnp.abs(log_lb - c))))
    lf_scr[...] = lf
    kk_scr[...] = one_m_lb * jnp.where(z >= 0, e, 1.0) / (1.0 + e)
    hi = lf.astype(BF16)
    r1 = lf - hi.astype(F32)
    mid = r1.astype(BF16)
    lo = (r1 - mid.astype(F32)).astype(BF16)
    tri = tri_ref[...]
    b_scr[...] = _dot(tri, hi) + _dot(tri, mid) + _dot(tri, lo)


def _level_operand(h, half, tm, q_scr, kk_scr, lf_scr, b_scr):
    hs = slice(h * HEAD_DIM, (h + 1) * HEAD_DIM)
    if half >= 8:
        blk = 2 * half
        pieces = []
        for r0 in range(0, tm, blk):
            bm = b_scr[r0 + half - 1:r0 + half, hs]
            pieces.append(kk_scr[r0:r0 + half, hs] * jnp.exp2(bm - b_scr[r0:r0 + half, hs]))
            pieces.append(q_scr[r0 + half:r0 + blk, hs] * jnp.exp2(b_scr[r0 + half:r0 + blk, hs] - bm))
        return jnp.concatenate(pieces, axis=0).astype(BF16)
    row = lax.broadcasted_iota(jnp.int32, (tm, HEAD_DIM), 0)
    first = (row & half) == 0
    lfh = lf_scr[:, hs]
    if half == 4:
        bh = b_scr[:, hs]
        bm = jnp.concatenate([jnp.broadcast_to(b_scr[r0 + 3:r0 + 4, hs], (8, HEAD_DIM))
                              for r0 in range(0, tm, 8)], axis=0)
        arg = jnp.where(first, bm - bh, bh - bm)
    elif half == 2:
        r = row & 3
        lf_next = pltpu.roll(lfh, tm - 1, 0)
        lf_prev = pltpu.roll(lfh, 1, 0)
        arg = jnp.where(r == 0, lf_next, jnp.where(r == 1, 0.0, jnp.where(r == 2, lfh, lfh + lf_prev)))
    else:
        arg = jnp.where(first, 0.0, lfh)
    return (jnp.where(first, kk_scr[:, hs], q_scr[:, hs]) * jnp.exp2(arg)).astype(BF16)


def _intra_head(h, seg, tm, proj_scr, q_scr, kk_scr, lf_scr, b_scr, lv_ref, fill=None):
    fill = fill or _Fillers(())
    hs = slice(h * HEAD_DIM, (h + 1) * HEAD_DIM)
    vh = proj_scr[:, OFF_I + h * HEAD_DIM:OFF_I + (h + 1) * HEAD_DIM]
    n_diag = tm // DIAG
    group = lv_ref.shape[1] // DIAG
    lv = lv_ref[...]
    diag = [None] * (n_diag // group)
    cross = {}
    zero = jnp.zeros((DIAG, HEAD_DIM), BF16)
    half = seg // 2
    while half >= 1:
        xl = _level_operand(h, half, tm, q_scr, kk_scr, lf_scr, b_scr)
        if half >= DIAG:
            for r0 in range(0, tm, 2 * half):
                for tq in range(r0 + half, r0 + 2 * half, DIAG):
                    for tk in range(r0, r0 + half, DIAG):
                        cross[(tq // DIAG, tk // DIAG)] = _dot_nt(xl[tq:tq + DIAG], xl[tk:tk + DIAG])
        else:
            lg = int(math.log2(half))
            for g in range(n_diag // group):
                xs = [xl[(g * group + d) * DIAG:(g * group + d + 1) * DIAG] for d in range(group)]
                if group == 1:
                    p = _dot_nt(xs[0], xs[0])
                else:
                    rhs = jnp.concatenate(
                        [jnp.concatenate([xs[d] if e == d else zero for e in range(group)], axis=1)
                         for d in range(group)], axis=0)
                    p = _dot_nt(jnp.concatenate(xs, axis=1), rhs)
                diag[g] = jnp.where(lv == lg, p, 0.0 if diag[g] is None else diag[g])
        fill.issue()
        half //= 2
    vb = vh.astype(BF16)
    outs = []
    for d in range(n_diag):
        own = diag[d // group][:, (d % group) * DIAG:(d % group + 1) * DIAG]
        p = jnp.concatenate([cross[(d, e)] for e in range(d)] + [own], axis=1).astype(BF16)
        outs.append(_dot(p, vb[0:(d + 1) * DIAG]))
    dg = jnp.sum(q_scr[:, hs] * kk_scr[:, hs], axis=-1, keepdims=True)
    return jnp.concatenate(outs, axis=0) + dg * vh


def _finish_head(h, o, proj_scr, vec_ref, o_scr):
    hs = slice(h * HEAD_DIM, (h + 1) * HEAD_DIM)
    g = proj_scr[:, OFF_G + h * HEAD_DIM:OFF_G + (h + 1) * HEAD_DIM]
    onorm = vec_ref[ROW_GATE:ROW_GATE + 1, HGRN_WIDTH + h * HEAD_DIM:HGRN_WIDTH + (h + 1) * HEAD_DIM]
    o = o * lax.rsqrt(jnp.mean(o * o, axis=-1, keepdims=True) + RMS_EPS)
    o_scr[:, hs] = o * onorm * (g * _sigmoid(g))


def _mix_norm(x, mixin, wout_ref, vec_ref):
    mix = _dot(mixin, wout_ref[...])
    return _layer_norm(ALPHA * x + mix, vec_ref[ROW_LN1G:ROW_LN1G + 1, :], vec_ref[ROW_LN1B:ROW_LN1B + 1, :])


def _ffn_chunk(c, x1b, w1_ref, w2_ref):
    cs = slice(c * FF_CHUNK, (c + 1) * FF_CHUNK)
    hid = jnp.maximum(_dot(x1b, w1_ref[:, cs]), 0.0)
    return _dot((hid * hid).astype(BF16), w2_ref[cs, :])


def _ffn_norm(x1, ffn, vec_ref):
    return _layer_norm(ALPHA * x1 + ffn, vec_ref[ROW_LN2G:ROW_LN2G + 1, :], vec_ref[ROW_LN2B:ROW_LN2B + 1, :])


def _back(x, mixin, wout_ref, w1_ref, w2_ref, vec_ref):
    x1 = _mix_norm(x, mixin, wout_ref, vec_ref)
    x1b = x1.astype(BF16)
    acc = _ffn_chunk(0, x1b, w1_ref, w2_ref)
    for c in range(1, N_FF_CHUNKS):
        acc = acc + _ffn_chunk(c, x1b, w1_ref, w2_ref)
    return _ffn_norm(x1, acc, vec_ref)


def _conv_weights(vec_ref):
    w0 = vec_ref[ROW_CW01:ROW_CW01 + 1, 0:CONV_WIDTH]
    w1 = vec_ref[ROW_CW01:ROW_CW01 + 1, CONV_WIDTH:2 * CONV_WIDTH]
    w2 = vec_ref[ROW_CW2:ROW_CW2 + 1, 0:CONV_WIDTH]
    return w0, w1, w2


def _prompt_kernel(x_ref, win_ref, wout_ref, w1_ref, w2_ref, vec_ref, lv_ref, tri_ref,
                   y_ref, hst_ref, cst_ref,
                   proj_scr, q_scr, kk_scr, lf_scr, b_scr, u_scr, o_scr, xs_scr, mix_scr,
                   x1_scr, x1b_scr, acc_scr, hid_scr, *, n_tiles, tiles_per_seq):
    tm = PROMPT_TILE
    i = pl.program_id(0)
    live = i < n_tiles
    j = jnp.minimum(i, n_tiles - 1) % tiles_per_seq

    @pl.when(i == 0)
    def _():
        xs_scr[...] = jnp.zeros(xs_scr.shape, F32)
        mix_scr[...] = jnp.zeros(mix_scr.shape, BF16)

    @pl.when(jnp.logical_and(j == 0, live))
    def _():
        hst_ref[...] = jnp.zeros(hst_ref.shape, F32)
        u_scr[0:8, :] = jnp.zeros((8, CONV_WIDTH), F32)

    x1 = _mix_norm(xs_scr[...], mix_scr[...], wout_ref, vec_ref)
    x1_scr[...] = x1
    x1b_scr[...] = x1.astype(BF16)

    x = x_ref[0]

    def conv_fn():
        u = proj_scr[:, OFF_C:OFF_C + CONV_WIDTH] * proj_scr[:, OFF_H:OFF_H + CONV_WIDTH]
        u_scr[8:8 + tm, :] = u
        w0, w1, w2 = _conv_weights(vec_ref)
        conv = w0 * u_scr[6:6 + tm, :] + w1 * u_scr[7:7 + tm, :] + w2 * u
        mix_scr[:, HGRN_WIDTH:HGRN_WIDTH + CONV_WIDTH] = (proj_scr[:, OFF_B:OFF_B + CONV_WIDTH] * conv).astype(BF16)
        tail = u_scr[tm:tm + 8, :]
        u_scr[0:8, :] = tail
        cst_ref[0] = tail[6:8, :]

    def ffn_fillers(c):
        def up(j):
            cols = slice(c * FF_CHUNK + j * FF_SUB, c * FF_CHUNK + (j + 1) * FF_SUB)
            hid = jnp.maximum(_dot(x1b_scr[...], w1_ref[:, cols]), 0.0)
            hid_scr[:, j * FF_SUB:(j + 1) * FF_SUB] = (hid * hid).astype(BF16)

        def down(j):
            cols = slice(j * FF_SUB, (j + 1) * FF_SUB)
            part = _dot(hid_scr[...], w2_ref[c * FF_CHUNK:(c + 1) * FF_CHUNK, cols])
            acc_scr[:, cols] = part if c == 0 else acc_scr[:, cols] + part

        return ([functools.partial(up, j) for j in range(FF_CHUNK // FF_SUB)]
                + [functools.partial(down, j) for j in range(D_MODEL // FF_SUB)])

    fill = _Fillers([f for c in range(N_FF_CHUNKS) for f in ffn_fillers(c)])
    _front(x, win_ref, vec_ref, tri_ref, proj_scr, q_scr, kk_scr, lf_scr, b_scr, conv_fn)

    for h in range(N_HEADS):
        hs = slice(h * HEAD_DIM, (h + 1) * HEAD_DIM)
        o = _intra_head(h, tm, tm, proj_scr, q_scr, kk_scr, lf_scr, b_scr, lv_ref, fill)
        qh = q_scr[:, hs]
        kh = kk_scr[:, hs]
        bh = b_scr[:, hs]
        vh = proj_scr[:, OFF_I + h * HEAD_DIM:OFF_I + (h + 1) * HEAD_DIM]
        s_prev = hst_ref[0, h]
        o = o + _dot((qh * jnp.exp2(bh)).astype(BF16), s_prev.astype(BF16))
        b_last = b_scr[tm - 1:tm, hs]
        kd = (kh * jnp.exp2(b_last - bh)).astype(BF16)
        upd = _dot_tn(kd, vh.astype(BF16))
        a_col = jnp.exp2(jnp.transpose(b_scr[tm - 8:tm, hs])[:, 7:8])
        hst_ref[0, h] = jnp.where(live, a_col * s_prev + upd, s_prev)
        _finish_head(h, o, proj_scr, vec_ref, o_scr)

    fill.flush()
    y_ref[0] = _ffn_norm(x1_scr[...], acc_scr[...], vec_ref)
    xs_scr[...] = x
    mix_scr[:, 0:HGRN_WIDTH] = o_scr[...].astype(BF16)


def _sample_kernel(x_ref, sin_ref, ext_ref, win_ref, wout_ref, w1_ref, w2_ref, vec_ref, lv_ref, tri_ref,
                   y_ref, sout_ref, u_ref,
                   proj_scr, q_scr, kk_scr, lf_scr, b_scr, o_scr, qe_scr, kd_scr, vb_scr, ys_scr):
    seg = 8
    tm = SAMPLE_SEQS * seg
    rows = pl.ds(pl.multiple_of(pl.program_id(1) * tm, tm), tm)

    @pl.when(pl.program_id(0) == 0)
    def _():
        ys_scr[rows, :] = x_ref[...]

    x = ys_scr[rows, :]
    _front(x, win_ref, vec_ref, tri_ref, proj_scr, q_scr, kk_scr, lf_scr, b_scr)

    for h in range(N_HEADS):
        hs = slice(h * HEAD_DIM, (h + 1) * HEAD_DIM)
        o_scr[:, hs] = _intra_head(h, seg, tm, proj_scr, q_scr, kk_scr, lf_scr, b_scr, lv_ref)
        bh = b_scr[:, hs]
        pieces = [jnp.broadcast_to(b_scr[pl.ds(seg * jb + seg - 1, 1), hs], (seg, HEAD_DIM))
                  for jb in range(tm // seg)]
        b_last = jnp.concatenate(pieces, axis=0)
        qe_scr[:, hs] = (q_scr[:, hs] * jnp.exp2(bh)).astype(BF16)
        kd_scr[:, hs] = (kk_scr[:, hs] * jnp.exp2(b_last - bh)).astype(BF16)
    vb_scr[...] = proj_scr[:, OFF_I:OFF_I + HGRN_WIDTH].astype(BF16)

    top = lax.broadcasted_iota(jnp.int32, (2 * seg, HEAD_DIM), 0) < seg
    for h in range(N_HEADS):
        hs = slice(h * HEAD_DIM, (h + 1) * HEAD_DIM)
        b_last = jnp.concatenate([b_scr[r:r + 1, hs] for r in range(seg - 1, tm, seg)], axis=0)
        a_cols = jnp.exp2(jnp.transpose(b_last))
        for p in range(SAMPLE_SEQS // 2):
            rows16 = slice(p * 2 * seg, (p + 1) * 2 * seg)
            qe = qe_scr[rows16, hs]
            kd = kd_scr[rows16, hs]
            vv = vb_scr[rows16, hs]
            s_a = sin_ref[2 * p, h]
            s_b = sin_ref[2 * p + 1, h]
            o_ab = _dot(qe, jnp.concatenate([s_a, s_b], axis=1).astype(BF16))
            o_scr[rows16, hs] = o_scr[rows16, hs] + jnp.where(top, o_ab[:, 0:HEAD_DIM], o_ab[:, HEAD_DIM:2 * HEAD_DIM])
            zero = jnp.zeros_like(kd)
            upd = _dot_tn(jnp.concatenate([jnp.where(top, kd, zero), jnp.where(top, zero, kd)], axis=1), vv)
            sout_ref[2 * p, h] = a_cols[:, 2 * p:2 * p + 1] * s_a + upd[0:HEAD_DIM]
            sout_ref[2 * p + 1, h] = a_cols[:, 2 * p + 1:2 * p + 2] * s_b + upd[HEAD_DIM:2 * HEAD_DIM]

    for h in range(N_HEADS):
        hs = slice(h * HEAD_DIM, (h + 1) * HEAD_DIM)
        _finish_head(h, o_scr[:, hs], proj_scr, vec_ref, o_scr)

    u = proj_scr[:, OFF_C:OFF_C + CONV_WIDTH] * proj_scr[:, OFF_H:OFF_H + CONV_WIDTH]
    u_ref[...] = u
    ext = ext_ref[...]
    tmod = lax.broadcasted_iota(jnp.int32, (tm, CONV_WIDTH), 0) & (seg - 1)
    u1 = jnp.where(tmod == 0, pltpu.roll(ext, tm - 1, 0), pltpu.roll(u, 1, 0))
    u2 = jnp.where(tmod < 2, ext, pltpu.roll(u, 2, 0))
    w0, w1, w2 = _conv_weights(vec_ref)
    yc = proj_scr[:, OFF_B:OFF_B + CONV_WIDTH] * (w0 * u2 + w1 * u1 + w2 * u)

    mixin = jnp.concatenate([o_scr[...], yc], axis=-1).astype(BF16)
    y = _back(x, mixin, wout_ref, w1_ref, w2_ref, vec_ref)
    ys_scr[rows, :] = y
    y_ref[...] = y


def _level_table(tm, seg):
    t = np.arange(tm)[:, None]
    s = np.arange(tm)[None, :]
    x = np.maximum(t ^ s, 1)
    lv = np.floor(np.log2(x)).astype(np.int32)
    ok = (s < t) & ((t // seg) == (s // seg))
    return np.where(ok, lv, -1).astype(np.int32)


def _tri_table(tm, seg):
    t = np.arange(tm)[:, None]
    s = np.arange(tm)[None, :]
    return ((s <= t) & ((t // seg) == (s // seg))).astype(np.float32)


def _const_spec(shape):
    nd = len(shape)
    return pl.BlockSpec(shape, lambda *_: (0,) * nd, pipeline_mode=pl.Buffered(1))


def _weight_specs(layer_of):
    def wspec(k, n):
        return pl.BlockSpec((None, k, n), lambda *g: (layer_of(*g), 0, 0), pipeline_mode=pl.Buffered(1))
    return [wspec(D_MODEL, PROJ_WIDTH), wspec(D_MODEL, D_MODEL), wspec(D_MODEL, D_FF), wspec(D_FF, D_MODEL),
            wspec(8, D_MODEL)]


def _prompt_layer(layer, x, weights, vecs):
    bsz, seq, _ = x.shape
    tm = PROMPT_TILE
    tiles_per_seq = seq // tm
    n_tiles = bsz * tiles_per_seq
    lv = jnp.asarray(np.tile(_level_table(DIAG, DIAG), (1, SCORE_GROUP)))
    tri = jnp.asarray(_tri_table(tm, tm), dtype=BF16)

    def cur(i):
        t = jnp.minimum(i, n_tiles - 1)
        return t // tiles_per_seq, t % tiles_per_seq

    def prev(i):
        t = jnp.maximum(i - 1, 0)
        return t // tiles_per_seq, t % tiles_per_seq

    return pl.pallas_call(
        functools.partial(_prompt_kernel, n_tiles=n_tiles, tiles_per_seq=tiles_per_seq),
        grid=(n_tiles + 1,),
        in_specs=[pl.BlockSpec((1, tm, D_MODEL), lambda i: (*cur(i), 0))] + _weight_specs(lambda i: layer)
        + [_const_spec((DIAG, SCORE_GROUP * DIAG)), _const_spec((tm, tm))],
        out_specs=[pl.BlockSpec((1, tm, D_MODEL), lambda i: (*prev(i), 0)),
                   pl.BlockSpec((1, N_HEADS, HEAD_DIM, HEAD_DIM), lambda i: (cur(i)[0], 0, 0, 0)),
                   pl.BlockSpec((1, CONV_K - 1, CONV_WIDTH), lambda i: (cur(i)[0], 0, 0))],
        out_shape=[jax.ShapeDtypeStruct((bsz, seq, D_MODEL), F32),
                   jax.ShapeDtypeStruct((bsz, N_HEADS, HEAD_DIM, HEAD_DIM), F32),
                   jax.ShapeDtypeStruct((bsz, CONV_K - 1, CONV_WIDTH), F32)],
        scratch_shapes=[pltpu.VMEM((tm, PROJ_WIDTH), F32),
                        pltpu.VMEM((tm, HGRN_WIDTH), F32), pltpu.VMEM((tm, HGRN_WIDTH), F32),
                        pltpu.VMEM((tm, HGRN_WIDTH), F32), pltpu.VMEM((tm, HGRN_WIDTH), F32),
                        pltpu.VMEM((tm + 8, CONV_WIDTH), F32), pltpu.VMEM((tm, HGRN_WIDTH), F32),
                        pltpu.VMEM((tm, D_MODEL), F32), pltpu.VMEM((tm, D_MODEL), BF16),
                        pltpu.VMEM((tm, D_MODEL), F32), pltpu.VMEM((tm, D_MODEL), BF16),
                        pltpu.VMEM((tm, D_MODEL), F32), pltpu.VMEM((tm, FF_CHUNK), BF16)],
        compiler_params=pltpu.CompilerParams(dimension_semantics=("arbitrary",),
                                             vmem_limit_bytes=V7X_VMEM_LIMIT),
        name=f"prompt_layer{layer}",
    )(x, *weights, vecs, lv, tri)


def _sample_layers(x, state, ext, weights, vecs):
    rows = x.shape[0]
    seg = 8
    tm = SAMPLE_SEQS * seg
    n_seq = rows // seg
    n_steps = n_seq // SAMPLE_SEQS
    lv = jnp.asarray(_level_table(DIAG, seg))
    tri = jnp.asarray(_tri_table(tm, seg), dtype=BF16)
    st_spec = pl.BlockSpec((None, SAMPLE_SEQS, N_HEADS, HEAD_DIM, HEAD_DIM), lambda l, i: (l, i, 0, 0, 0))
    u_spec = pl.BlockSpec((None, tm, CONV_WIDTH), lambda l, i: (l, i, 0))
    return pl.pallas_call(
        _sample_kernel,
        grid=(DEPTH, n_steps),
        in_specs=[pl.BlockSpec((tm, D_MODEL), lambda l, i: (i * (1 - l) + (n_steps - 1) * l, 0)), st_spec, u_spec]
        + _weight_specs(lambda l, i: l) + [_const_spec((DIAG, DIAG)), _const_spec((tm, tm))],
        out_specs=[pl.BlockSpec((tm, D_MODEL), lambda l, i: (i * l, 0)), st_spec, u_spec],
        out_shape=[jax.ShapeDtypeStruct((rows, D_MODEL), F32),
                   jax.ShapeDtypeStruct((DEPTH, n_seq, N_HEADS, HEAD_DIM, HEAD_DIM), F32),
                   jax.ShapeDtypeStruct((DEPTH, rows, CONV_WIDTH), F32)],
        scratch_shapes=[pltpu.VMEM((tm, PROJ_WIDTH), F32),
                        pltpu.VMEM((tm, HGRN_WIDTH), F32), pltpu.VMEM((tm, HGRN_WIDTH), F32),
                        pltpu.VMEM((tm, HGRN_WIDTH), F32), pltpu.VMEM((tm, HGRN_WIDTH), F32),
                        pltpu.VMEM((tm, HGRN_WIDTH), F32),
                        pltpu.VMEM((tm, HGRN_WIDTH), BF16), pltpu.VMEM((tm, HGRN_WIDTH), BF16),
                        pltpu.VMEM((tm, HGRN_WIDTH), BF16), pltpu.VMEM((rows, D_MODEL), F32)],
        compiler_params=pltpu.CompilerParams(dimension_semantics=("arbitrary", "arbitrary"),
                                             vmem_limit_bytes=V7X_VMEM_LIMIT),
        name="sample_layers",
    )(x, state, ext, *weights, vecs, lv, tri)


def kernel(x_prompt, x_sample, state_hgrn, state_conv, w_in, lb_logits, conv_w, onorm_g, w_out,
           ln1_g, ln1_b, w_ff1, w_ff2, ln2_g, ln2_b):
    n_seq, dec_seq, _ = x_sample.shape
    p = jax.nn.softmax(lb_logits.astype(F32), axis=0)
    cum = jnp.cumsum(p, axis=0)
    lb = cum - cum[0:1]
    zeros_half = jnp.zeros((DEPTH, CONV_WIDTH), F32)
    vecs = jnp.stack([
        ln1_g, ln1_b, ln2_g, ln2_b,
        jnp.concatenate([jnp.log(lb), jnp.log1p(-lb)], axis=-1),
        jnp.concatenate([1.0 - lb, onorm_g], axis=-1),
        jnp.concatenate([conv_w[:, 0], conv_w[:, 1]], axis=-1),
        jnp.concatenate([conv_w[:, 2], zeros_half], axis=-1)], axis=1).astype(F32)
    weights = (w_in.astype(BF16), w_out.astype(BF16), w_ff1.astype(BF16), w_ff2.astype(BF16))
    ext = jnp.pad(state_conv, ((0, 0), (0, 0), (0, dec_seq - (CONV_K - 1)), (0, 0)))
    ext = ext.reshape(DEPTH, n_seq * dec_seq, CONV_WIDTH)

    yp = x_prompt
    ys = x_sample.reshape(n_seq * dec_seq, D_MODEL)
    hp, cp = [], []
    for layer in range(DEPTH):
        yp, hgrn_p, conv_p = _prompt_layer(layer, yp, weights, vecs)
        hp.append(hgrn_p)
        cp.append(conv_p)
    ys, hgrn_s, u_s = _sample_layers(ys, state_hgrn, ext, weights, vecs)
    conv_s = u_s.reshape(DEPTH, n_seq, dec_seq, CONV_WIDTH)[:, :, dec_seq - (CONV_K - 1):]
    return (yp, ys.reshape(n_seq, dec_seq, D_MODEL), jnp.stack(hp), jnp.stack(cp), hgrn_s, conv_s)
```

```python
import functools
import math

import numpy as np
import jax
import jax.numpy as jnp
from jax import lax
from jax.experimental import pallas as pl
from jax.experimental.pallas import tpu as pltpu

F32 = jnp.float32
BF16 = jnp.bfloat16

D_MODEL = 1024
DEPTH = 2
HGRN_WIDTH = 512
CONV_WIDTH = 512
HEAD_DIM = 128
N_HEADS = HGRN_WIDTH // HEAD_DIM
CONV_K = 3
D_FF = 4 * D_MODEL
PROJ_WIDTH = 4 * HGRN_WIDTH + 3 * CONV_WIDTH
FF_CHUNK = 1024
N_FF_CHUNKS = D_FF // FF_CHUNK
ALPHA = float((2 * DEPTH) ** 0.25)
LN_EPS = 1e-5
RMS_EPS = 1e-6

OFF_Q, OFF_F, OFF_I, OFF_G = 0, HGRN_WIDTH, 2 * HGRN_WIDTH, 3 * HGRN_WIDTH
OFF_B = 4 * HGRN_WIDTH
OFF_C = OFF_B + CONV_WIDTH
OFF_H = OFF_C + CONV_WIDTH

LOG2_E = math.log2(math.e)
DIAG = 128
SCORE_GROUP = 2
FF_SUB = 256
PROMPT_TILE = 256
SAMPLE_SEQS = 16
V7X_VMEM_LIMIT = 58 * 1024 * 1024

ROW_LN1G, ROW_LN1B, ROW_LN2G, ROW_LN2B, ROW_LOGLB, ROW_GATE, ROW_CW01, ROW_CW2 = range(8)


def _dot(a, b):
    return jnp.dot(a, b, preferred_element_type=F32)


def _dot_nt(a, b):
    return lax.dot_general(a, b, (((1,), (1,)), ((), ())), preferred_element_type=F32)


def _dot_tn(a, b):
    return lax.dot_general(a, b, (((0,), (0,)), ((), ())), preferred_element_type=F32)


def _sigmoid(x):
    return 1.0 / (1.0 + jnp.exp(-x))


def _layer_norm(y, g, b):
    mu = jnp.mean(y, axis=-1, keepdims=True)
    yc = y - mu
    var = jnp.mean(yc * yc, axis=-1, keepdims=True)
    return yc * lax.rsqrt(var + LN_EPS) * g + b


class _Fillers:
    def __init__(self, thunks):
        self._thunks = list(thunks)

    def issue(self):
        if self._thunks:
            self._thunks.pop(0)()

    def flush(self):
        while self._thunks:
            self._thunks.pop(0)()


def _front(x, win_ref, vec_ref, tri_ref, proj_scr, q_scr, kk_scr, lf_scr, b_scr, conv_fn=None):
    xb = x.astype(BF16)
    proj_scr[:, OFF_B:PROJ_WIDTH] = _dot(xb, win_ref[:, OFF_B:PROJ_WIDTH])
    proj_scr[:, OFF_Q:OFF_I] = _dot(xb, win_ref[:, OFF_Q:OFF_I])
    if conv_fn is not None:
        conv_fn()
    proj_scr[:, OFF_I:OFF_B] = _dot(xb, win_ref[:, OFF_I:OFF_B])
    qp = proj_scr[:, OFF_Q:OFF_Q + HGRN_WIDTH]
    q_scr[...] = qp * _sigmoid(qp)
    z = proj_scr[:, OFF_F:OFF_F + HGRN_WIDTH]
    log_lb = vec_ref[ROW_LOGLB:ROW_LOGLB + 1, 0:HGRN_WIDTH]
    log_1mlb = vec_ref[ROW_LOGLB:ROW_LOGLB + 1, HGRN_WIDTH:2 * HGRN_WIDTH]
    one_m_lb = vec_ref[ROW_GATE:ROW_GATE + 1, 0:HGRN_WIDTH]
    e = jnp.exp(-jnp.abs(z))
    log_sig = jnp.minimum(z, 0.0) - jnp.log1p(e)
    c = log_1mlb + log_sig
    lf = LOG2_E * (jnp.maximum(log_lb, c) + jnp.log1p(jnp.exp(-jnp.abs(log_lb - c))))
    lf_scr[...] = lf
    kk_scr[...] = one_m_lb * jnp.where(z >= 0, e, 1.0) / (1.0 + e)
    hi = lf.astype(BF16)
    r1 = lf - hi.astype(F32)
    mid = r1.astype(BF16)
    lo = (r1 - mid.astype(F32)).astype(BF16)
    parts = _dot(tri_ref[...], jnp.concatenate([hi, mid, lo], axis=1))
    b_scr[...] = (parts[:, 0:HGRN_WIDTH] + parts[:, HGRN_WIDTH:2 * HGRN_WIDTH]
                  + parts[:, 2 * HGRN_WIDTH:3 * HGRN_WIDTH])


def _level_operand(h, half, tm, q_scr, kk_scr, lf_scr, b_scr):
    hs = slice(h * HEAD_DIM, (h + 1) * HEAD_DIM)
    if half >= 8:
        blk = 2 * half
        pieces = []
        for r0 in range(0, tm, blk):
            bm = b_scr[r0 + half - 1:r0 + half, hs]
            pieces.append(kk_scr[r0:r0 + half, hs] * jnp.exp2(bm - b_scr[r0:r0 + half, hs]))
            pieces.append(q_scr[r0 + half:r0 + blk, hs] * jnp.exp2(b_scr[r0 + half:r0 + blk, hs] - bm))
        return jnp.concatenate(pieces, axis=0).astype(BF16)
    row = lax.broadcasted_iota(jnp.int32, (tm, HEAD_DIM), 0)
    first = (row & half) == 0
    lfh = lf_scr[:, hs]
    if half == 4:
        bh = b_scr[:, hs]
        bm = jnp.concatenate([jnp.broadcast_to(b_scr[r0 + 3:r0 + 4, hs], (8, HEAD_DIM))
                              for r0 in range(0, tm, 8)], axis=0)
        arg = jnp.where(first, bm - bh, bh - bm)
    elif half == 2:
        r = row & 3
        lf_next = pltpu.roll(lfh, tm - 1, 0)
        lf_prev = pltpu.roll(lfh, 1, 0)
        arg = jnp.where(r == 0, lf_next, jnp.where(r == 1, 0.0, jnp.where(r == 2, lfh, lfh + lf_prev)))
    else:
        arg = jnp.where(first, 0.0, lfh)
    return (jnp.where(first, kk_scr[:, hs], q_scr[:, hs]) * jnp.exp2(arg)).astype(BF16)


def _intra_head(h, seg, tm, proj_scr, q_scr, kk_scr, lf_scr, b_scr, lv_ref, fill=None):
    fill = fill or _Fillers(())
    hs = slice(h * HEAD_DIM, (h + 1) * HEAD_DIM)
    vh = proj_scr[:, OFF_I + h * HEAD_DIM:OFF_I + (h + 1) * HEAD_DIM]
    n_diag = tm // DIAG
    group = lv_ref.shape[1] // DIAG
    lv = lv_ref[...]
    diag = [None] * (n_diag // group)
    cross = {}
    zero = jnp.zeros((DIAG, HEAD_DIM), BF16)
    half = seg // 2
    while half >= 1:
        xl = _level_operand(h, half, tm, q_scr, kk_scr, lf_scr, b_scr)
        if half >= DIAG:
            for r0 in range(0, tm, 2 * half):
                for tq in range(r0 + half, r0 + 2 * half, DIAG):
                    for tk in range(r0, r0 + half, DIAG):
                        cross[(tq // DIAG, tk // DIAG)] = _dot_nt(xl[tq:tq + DIAG], xl[tk:tk + DIAG])
        else:
            lg = int(math.log2(half))
            for g in range(n_diag // group):
                xs = [xl[(g * group + d) * DIAG:(g * group + d + 1) * DIAG] for d in range(group)]
                if group == 1:
                    p = _dot_nt(xs[0], xs[0])
                else:
                    rhs = jnp.concatenate(
                        [jnp.concatenate([xs[d] if e == d else zero for e in range(group)], axis=1)
                         for d in range(group)], axis=0)
                    p = _dot_nt(jnp.concatenate(xs, axis=1), rhs)
                diag[g] = jnp.where(lv == lg, p, 0.0 if diag[g] is None else diag[g])
        fill.issue()
        half //= 2
    blank = jnp.zeros((DIAG, DIAG), F32)
    rows = []
    for d in range(n_diag):
        own = diag[d // group][:, (d % group) * DIAG:(d % group + 1) * DIAG]
        rows.append(jnp.concatenate([cross[(d, e)] for e in range(d)] + [own] + [blank] * (n_diag - 1 - d), axis=1))
    scores = rows[0] if n_diag == 1 else jnp.concatenate(rows, axis=0)
    dg = jnp.sum(q_scr[:, hs] * kk_scr[:, hs], axis=-1, keepdims=True)
    return _dot(scores.astype(BF16), vh.astype(BF16)) + dg * vh


def _finish_head(h, o, proj_scr, vec_ref, o_scr):
    hs = slice(h * HEAD_DIM, (h + 1) * HEAD_DIM)
    g = proj_scr[:, OFF_G + h * HEAD_DIM:OFF_G + (h + 1) * HEAD_DIM]
    onorm = vec_ref[ROW_GATE:ROW_GATE + 1, HGRN_WIDTH + h * HEAD_DIM:HGRN_WIDTH + (h + 1) * HEAD_DIM]
    o = o * lax.rsqrt(jnp.mean(o * o, axis=-1, keepdims=True) + RMS_EPS)
    o_scr[:, hs] = o * onorm * (g * _sigmoid(g))


def _mix_norm(x, mixin, wout_ref, vec_ref):
    mix = _dot(mixin, wout_ref[...])
    return _layer_norm(ALPHA * x + mix, vec_ref[ROW_LN1G:ROW_LN1G + 1, :], vec_ref[ROW_LN1B:ROW_LN1B + 1, :])


def _ffn_chunk(c, x1b, w1_ref, w2_ref):
    cs = slice(c * FF_CHUNK, (c + 1) * FF_CHUNK)
    hid = jnp.maximum(_dot(x1b, w1_ref[:, cs]), 0.0)
    return _dot((hid * hid).astype(BF16), w2_ref[cs, :])


def _ffn_norm(x1, ffn, vec_ref):
    return _layer_norm(ALPHA * x1 + ffn, vec_ref[ROW_LN2G:ROW_LN2G + 1, :], vec_ref[ROW_LN2B:ROW_LN2B + 1, :])


def _back(x, mixin, wout_ref, w1_ref, w2_ref, vec_ref):
    x1 = _mix_norm(x, mixin, wout_ref, vec_ref)
    x1b = x1.astype(BF16)
    acc = _ffn_chunk(0, x1b, w1_ref, w2_ref)
    for c in range(1, N_FF_CHUNKS):
        acc = acc + _ffn_chunk(c, x1b, w1_ref, w2_ref)
    return _ffn_norm(x1, acc, vec_ref)


def _conv_weights(vec_ref):
    w0 = vec_ref[ROW_CW01:ROW_CW01 + 1, 0:CONV_WIDTH]
    w1 = vec_ref[ROW_CW01:ROW_CW01 + 1, CONV_WIDTH:2 * CONV_WIDTH]
    w2 = vec_ref[ROW_CW2:ROW_CW2 + 1, 0:CONV_WIDTH]
    return w0, w1, w2


def _prompt_kernel(x_ref, win_ref, wout_ref, w1_ref, w2_ref, vec_ref, lv_ref, tri_ref,
                   y_ref, hst_ref, cst_ref,
                   proj_scr, q_scr, kk_scr, lf_scr, b_scr, u_scr, o_scr, xs_scr, mix_scr,
                   x1_scr, x1b_scr, acc_scr, hid_scr, *, n_tiles, tiles_per_seq):
    tm = PROMPT_TILE
    i = pl.program_id(0)
    live = i < n_tiles
    j = jnp.minimum(i, n_tiles - 1) % tiles_per_seq

    @pl.when(i == 0)
    def _():
        xs_scr[...] = jnp.zeros(xs_scr.shape, F32)
        mix_scr[...] = jnp.zeros(mix_scr.shape, BF16)

    @pl.when(jnp.logical_and(j == 0, live))
    def _():
        hst_ref[...] = jnp.zeros(hst_ref.shape, F32)
        u_scr[0:8, :] = jnp.zeros((8, CONV_WIDTH), F32)

    x1 = _mix_norm(xs_scr[...], mix_scr[...], wout_ref, vec_ref)
    x1_scr[...] = x1
    x1b_scr[...] = x1.astype(BF16)

    x = x_ref[0]

    def conv_fn():
        u = proj_scr[:, OFF_C:OFF_C + CONV_WIDTH] * proj_scr[:, OFF_H:OFF_H + CONV_WIDTH]
        u_scr[8:8 + tm, :] = u
        w0, w1, w2 = _conv_weights(vec_ref)
        conv = w0 * u_scr[6:6 + tm, :] + w1 * u_scr[7:7 + tm, :] + w2 * u
        mix_scr[:, HGRN_WIDTH:HGRN_WIDTH + CONV_WIDTH] = (proj_scr[:, OFF_B:OFF_B + CONV_WIDTH] * conv).astype(BF16)
        tail = u_scr[tm:tm + 8, :]
        u_scr[0:8, :] = tail
        cst_ref[0] = tail[6:8, :]

    def ffn_fillers(c):
        def up(j):
            cols = slice(c * FF_CHUNK + j * FF_SUB, c * FF_CHUNK + (j + 1) * FF_SUB)
            hid = jnp.maximum(_dot(x1b_scr[...], w1_ref[:, cols]), 0.0)
            hid_scr[:, j * FF_SUB:(j + 1) * FF_SUB] = (hid * hid).astype(BF16)

        def down(j):
            cols = slice(j * FF_SUB, (j + 1) * FF_SUB)
            part = _dot(hid_scr[...], w2_ref[c * FF_CHUNK:(c + 1) * FF_CHUNK, cols])
            acc_scr[:, cols] = part if c == 0 else acc_scr[:, cols] + part

        return ([functools.partial(up, j) for j in range(FF_CHUNK // FF_SUB)]
                + [functools.partial(down, j) for j in range(D_MODEL // FF_SUB)])

    fill = _Fillers([f for c in range(N_FF_CHUNKS) for f in ffn_fillers(c)])
    _front(x, win_ref, vec_ref, tri_ref, proj_scr, q_scr, kk_scr, lf_scr, b_scr, conv_fn)

    for h in range(N_HEADS):
        hs = slice(h * HEAD_DIM, (h + 1) * HEAD_DIM)
        o = _intra_head(h, tm, tm, proj_scr, q_scr, kk_scr, lf_scr, b_scr, lv_ref, fill)
        qh = q_scr[:, hs]
        kh = kk_scr[:, hs]
        bh = b_scr[:, hs]
        vh = proj_scr[:, OFF_I + h * HEAD_DIM:OFF_I + (h + 1) * HEAD_DIM]
        s_prev = hst_ref[0, h]
        o = o + _dot((qh * jnp.exp2(bh)).astype(BF16), s_prev.astype(BF16))
        b_last = b_scr[tm - 1:tm, hs]
        kd = (kh * jnp.exp2(b_last - bh)).astype(BF16)
        upd = _dot_tn(kd, vh.astype(BF16))
        a_col = jnp.exp2(jnp.transpose(b_scr[tm - 8:tm, hs])[:, 7:8])
        hst_ref[0, h] = jnp.where(live, a_col * s_prev + upd, s_prev)
        _finish_head(h, o, proj_scr, vec_ref, o_scr)

    fill.flush()
    y_ref[0] = _ffn_norm(x1_scr[...], acc_scr[...], vec_ref)
    xs_scr[...] = x
    mix_scr[:, 0:HGRN_WIDTH] = o_scr[...].astype(BF16)


def _sample_kernel(x_ref, sin_ref, ext_ref, win_ref, wout_ref, w1_ref, w2_ref, vec_ref, lv_ref, tri_ref,
                   y_ref, sout_ref, u_ref,
                   proj_scr, q_scr, kk_scr, lf_scr, b_scr, o_scr, qe_scr, kd_scr, vb_scr, ys_scr):
    seg = 8
    tm = SAMPLE_SEQS * seg
    rows = pl.ds(pl.multiple_of(pl.program_id(1) * tm, tm), tm)

    @pl.when(pl.program_id(0) == 0)
    def _():
        ys_scr[rows, :] = x_ref[...]

    x = ys_scr[rows, :]
    _front(x, win_ref, vec_ref, tri_ref, proj_scr, q_scr, kk_scr, lf_scr, b_scr)

    for h in range(N_HEADS):
        hs = slice(h * HEAD_DIM, (h + 1) * HEAD_DIM)
        o_scr[:, hs] = _intra_head(h, seg, tm, proj_scr, q_scr, kk_scr, lf_scr, b_scr, lv_ref)
        bh = b_scr[:, hs]
        pieces = [jnp.broadcast_to(b_scr[pl.ds(seg * jb + seg - 1, 1), hs], (seg, HEAD_DIM))
                  for jb in range(tm // seg)]
        b_last = jnp.concatenate(pieces, axis=0)
        qe_scr[:, hs] = (q_scr[:, hs] * jnp.exp2(bh)).astype(BF16)
        kd_scr[:, hs] = (kk_scr[:, hs] * jnp.exp2(b_last - bh)).astype(BF16)
    vb_scr[...] = proj_scr[:, OFF_I:OFF_I + HGRN_WIDTH].astype(BF16)

    top = lax.broadcasted_iota(jnp.int32, (2 * seg, HEAD_DIM), 0) < seg
    for h in range(N_HEADS):
        hs = slice(h * HEAD_DIM, (h + 1) * HEAD_DIM)
        b_last = jnp.concatenate([b_scr[r:r + 1, hs] for r in range(seg - 1, tm, seg)], axis=0)
        a_cols = jnp.exp2(jnp.transpose(b_last))
        for p in range(SAMPLE_SEQS // 2):
            rows16 = slice(p * 2 * seg, (p + 1) * 2 * seg)
            qe = qe_scr[rows16, hs]
            kd = kd_scr[rows16, hs]
            vv = vb_scr[rows16, hs]
            s_a = sin_ref[2 * p, h]
            s_b = sin_ref[2 * p + 1, h]
            o_ab = _dot(qe, jnp.concatenate([s_a, s_b], axis=1).astype(BF16))
            o_scr[rows16, hs] = o_scr[rows16, hs] + jnp.where(top, o_ab[:, 0:HEAD_DIM], o_ab[:, HEAD_DIM:2 * HEAD_DIM])
            zero = jnp.zeros_like(kd)
            upd = _dot_tn(jnp.concatenate([jnp.where(top, kd, zero), jnp.where(top, zero, kd)], axis=1), vv)
            sout_ref[2 * p, h] = a_cols[:, 2 * p:2 * p + 1] * s_a + upd[0:HEAD_DIM]
            sout_ref[2 * p + 1, h] = a_cols[:, 2 * p + 1:2 * p + 2] * s_b + upd[HEAD_DIM:2 * HEAD_DIM]

    for h in range(N_HEADS):
        hs = slice(h * HEAD_DIM, (h + 1) * HEAD_DIM)
        _finish_head(h, o_scr[:, hs], proj_scr, vec_ref, o_scr)

    u = proj_scr[:, OFF_C:OFF_C + CONV_WIDTH] * proj_scr[:, OFF_H:OFF_H + CONV_WIDTH]
    u_ref[...] = u
    ext = ext_ref[...]
    tmod = lax.broadcasted_iota(jnp.int32, (tm, CONV_WIDTH), 0) & (seg - 1)
    u1 = jnp.where(tmod == 0, pltpu.roll(ext, tm - 1, 0), pltpu.roll(u, 1, 0))
    u2 = jnp.where(tmod < 2, ext, pltpu.roll(u, 2, 0))
    w0, w1, w2 = _conv_weights(vec_ref)
    yc = proj_scr[:, OFF_B:OFF_B + CONV_WIDTH] * (w0 * u2 + w1 * u1 + w2 * u)

    mixin = jnp.concatenate([o_scr[...], yc], axis=-1).astype(BF16)
    y = _back(x, mixin, wout_ref, w1_ref, w2_ref, vec_ref)
    ys_scr[rows, :] = y
    y_ref[...] = y


def _level_table(tm, seg):
    t = np.arange(tm)[:, None]
    s = np.arange(tm)[None, :]
    x = np.maximum(t ^ s, 1)
    lv = np.floor(np.log2(x)).astype(np.int32)
    ok = (s < t) & ((t // seg) == (s // seg))
    return np.where(ok, lv, -1).astype(np.int32)


def _tri_table(tm, seg):
    t = np.arange(tm)[:, None]
    s = np.arange(tm)[None, :]
    return ((s <= t) & ((t // seg) == (s // seg))).astype(np.float32)


def _const_spec(shape):
    nd = len(shape)
    return pl.BlockSpec(shape, lambda *_: (0,) * nd, pipeline_mode=pl.Buffered(1))


def _weight_specs(layer_of):
    def wspec(k, n):
        return pl.BlockSpec((None, k, n), lambda *g: (layer_of(*g), 0, 0), pipeline_mode=pl.Buffered(1))
    return [wspec(D_MODEL, PROJ_WIDTH), wspec(D_MODEL, D_MODEL), wspec(D_MODEL, D_FF), wspec(D_FF, D_MODEL),
            wspec(8, D_MODEL)]


def _prompt_layer(layer, x, weights, vecs):
    bsz, seq, _ = x.shape
    tm = PROMPT_TILE
    tiles_per_seq = seq // tm
    n_tiles = bsz * tiles_per_seq
    lv = jnp.asarray(np.tile(_level_table(DIAG, DIAG), (1, SCORE_GROUP)))
    tri = jnp.asarray(_tri_table(tm, tm), dtype=BF16)

    def cur(i):
        t = jnp.minimum(i, n_tiles - 1)
        return t // tiles_per_seq, t % tiles_per_seq

    def prev(i):
        t = jnp.maximum(i - 1, 0)
        return t // tiles_per_seq, t % tiles_per_seq

    return pl.pallas_call(
        functools.partial(_prompt_kernel, n_tiles=n_tiles, tiles_per_seq=tiles_per_seq),
        grid=(n_tiles + 1,),
        in_specs=[pl.BlockSpec((1, tm, D_MODEL), lambda i: (*cur(i), 0))] + _weight_specs(lambda i: layer)
        + [_const_spec((DIAG, SCORE_GROUP * DIAG)), _const_spec((tm, tm))],
        out_specs=[pl.BlockSpec((1, tm, D_MODEL), lambda i: (*prev(i), 0)),
                   pl.BlockSpec((1, N_HEADS, HEAD_DIM, HEAD_DIM), lambda i: (cur(i)[0], 0, 0, 0)),
                   pl.BlockSpec((1, CONV_K - 1, CONV_WIDTH), lambda i: (cur(i)[0], 0, 0))],
        out_shape=[jax.ShapeDtypeStruct((bsz, seq, D_MODEL), F32),
                   jax.ShapeDtypeStruct((bsz, N_HEADS, HEAD_DIM, HEAD_DIM), F32),
                   jax.ShapeDtypeStruct((bsz, CONV_K - 1, CONV_WIDTH), F32)],
        scratch_shapes=[pltpu.VMEM((tm, PROJ_WIDTH), F32),
                        pltpu.VMEM((tm, HGRN_WIDTH), F32), pltpu.VMEM((tm, HGRN_WIDTH), F32),
                        pltpu.VMEM((tm, HGRN_WIDTH), F32), pltpu.VMEM((tm, HGRN_WIDTH), F32),
                        pltpu.VMEM((tm + 8, CONV_WIDTH), F32), pltpu.VMEM((tm, HGRN_WIDTH), F32),
                        pltpu.VMEM((tm, D_MODEL), F32), pltpu.VMEM((tm, D_MODEL), BF16),
                        pltpu.VMEM((tm, D_MODEL), F32), pltpu.VMEM((tm, D_MODEL), BF16),
                        pltpu.VMEM((tm, D_MODEL), F32), pltpu.VMEM((tm, FF_CHUNK), BF16)],
        compiler_params=pltpu.CompilerParams(dimension_semantics=("arbitrary",),
                                             vmem_limit_bytes=V7X_VMEM_LIMIT),
        name=f"prompt_layer{layer}",
    )(x, *weights, vecs, lv, tri)


def _sample_layers(x, state, ext, weights, vecs):
    rows = x.shape[0]
    seg = 8
    tm = SAMPLE_SEQS * seg
    n_seq = rows // seg
    n_steps = n_seq // SAMPLE_SEQS
    lv = jnp.asarray(_level_table(DIAG, seg))
    tri = jnp.asarray(_tri_table(tm, seg), dtype=BF16)
    st_spec = pl.BlockSpec((None, SAMPLE_SEQS, N_HEADS, HEAD_DIM, HEAD_DIM), lambda l, i: (l, i, 0, 0, 0))
    u_spec = pl.BlockSpec((None, tm, CONV_WIDTH), lambda l, i: (l, i, 0))
    return pl.pallas_call(
        _sample_kernel,
        grid=(DEPTH, n_steps),
        in_specs=[pl.BlockSpec((tm, D_MODEL), lambda l, i: (i * (1 - l) + (n_steps - 1) * l, 0)), st_spec, u_spec]
        + _weight_specs(lambda l, i: l) + [_const_spec((DIAG, DIAG)), _const_spec((tm, tm))],
        out_specs=[pl.BlockSpec((tm, D_MODEL), lambda l, i: (i * l, 0)), st_spec, u_spec],
        out_shape=[jax.ShapeDtypeStruct((rows, D_MODEL), F32),
                   jax.ShapeDtypeStruct((DEPTH, n_seq, N_HEADS, HEAD_DIM, HEAD_DIM), F32),
                   jax.ShapeDtypeStruct((DEPTH, rows, CONV_WIDTH), F32)],
        scratch_shapes=[pltpu.VMEM((tm, PROJ_WIDTH), F32),
                        pltpu.VMEM((tm, HGRN_WIDTH), F32), pltpu.VMEM((tm, HGRN_WIDTH), F32),
                        pltpu.VMEM((tm, HGRN_WIDTH), F32), pltpu.VMEM((tm, HGRN_WIDTH), F32),
                        pltpu.VMEM((tm, HGRN_WIDTH), F32),
                        pltpu.VMEM((tm, HGRN_WIDTH), BF16), pltpu.VMEM((tm, HGRN_WIDTH), BF16),
                        pltpu.VMEM((tm, HGRN_WIDTH), BF16), pltpu.VMEM((rows, D_MODEL), F32)],
        compiler_params=pltpu.CompilerParams(dimension_semantics=("arbitrary", "arbitrary"),
                                             vmem_limit_bytes=V7X_VMEM_LIMIT),
        name="sample_layers",
    )(x, state, ext, *weights, vecs, lv, tri)


def kernel(x_prompt, x_sample, state_hgrn, state_conv, w_in, lb_logits, conv_w, onorm_g, w_out,
           ln1_g, ln1_b, w_ff1, w_ff2, ln2_g, ln2_b):
    n_seq, dec_seq, _ = x_sample.shape
    p = jax.nn.softmax(lb_logits.astype(F32), axis=0)
    cum = jnp.cumsum(p, axis=0)
    lb = cum - cum[0:1]
    zeros_half = jnp.zeros((DEPTH, CONV_WIDTH), F32)
    vecs = jnp.stack([
        ln1_g, ln1_b, ln2_g, ln2_b,
        jnp.concatenate([jnp.log(lb), jnp.log1p(-lb)], axis=-1),
        jnp.concatenate([1.0 - lb, onorm_g], axis=-1),
        jnp.concatenate([conv_w[:, 0], conv_w[:, 1]], axis=-1),
        jnp.concatenate([conv_w[:, 2], zeros_half], axis=-1)], axis=1).astype(F32)
    weights = (w_in.astype(BF16), w_out.astype(BF16), w_ff1.astype(BF16), w_ff2.astype(BF16))
    ext = jnp.pad(state_conv, ((0, 0), (0, 0), (0, dec_seq - (CONV_K - 1)), (0, 0)))
    ext = ext.reshape(DEPTH, n_seq * dec_seq, CONV_WIDTH)

    yp = x_prompt
    ys = x_sample.reshape(n_seq * dec_seq, D_MODEL)
    hp, cp = [], []
    for layer in range(DEPTH):
        yp, hgrn_p, conv_p = _prompt_layer(layer, yp, weights, vecs)
        hp.append(hgrn_p)
        cp.append(conv_p)
    ys, hgrn_s, u_s = _sample_layers(ys, state_hgrn, ext, weights, vecs)
    conv_s = u_s.reshape(DEPTH, n_seq, dec_seq, CONV_WIDTH)[:, :, dec_seq - (CONV_K - 1):]
    return (yp, ys.reshape(n_seq, dec_seq, D_MODEL), jnp.stack(hp), jnp.stack(cp), hgrn_s, conv_s)
```

```python
import functools
import math

import numpy as np
import jax
import jax.numpy as jnp
from jax import lax
from jax.experimental import pallas as pl
from jax.experimental.pallas import tpu as pltpu

F32 = jnp.float32
BF16 = jnp.bfloat16

D_MODEL = 1024
DEPTH = 2
HGRN_WIDTH = 512
CONV_WIDTH = 512
HEAD_DIM = 128
N_HEADS = HGRN_WIDTH // HEAD_DIM
CONV_K = 3
D_FF = 4 * D_MODEL
PROJ_WIDTH = 4 * HGRN_WIDTH + 3 * CONV_WIDTH
FF_CHUNK = 1024
N_FF_CHUNKS = D_FF // FF_CHUNK
ALPHA = float((2 * DEPTH) ** 0.25)
LN_EPS = 1e-5
RMS_EPS = 1e-6

OFF_Q, OFF_F, OFF_I, OFF_G = 0, HGRN_WIDTH, 2 * HGRN_WIDTH, 3 * HGRN_WIDTH
OFF_B = 4 * HGRN_WIDTH
OFF_C = OFF_B + CONV_WIDTH
OFF_H = OFF_C + CONV_WIDTH

LOG2_E = math.log2(math.e)
DIAG = 128
SCORE_GROUP = 2
FF_SUB = 512
FILL_EVERY = 2
PROMPT_TILE = 256
SAMPLE_SEQS = 16
V7X_VMEM_LIMIT = 58 * 1024 * 1024

ROW_LN1G, ROW_LN1B, ROW_LN2G, ROW_LN2B, ROW_LOGLB, ROW_GATE, ROW_CW01, ROW_CW2 = range(8)


def _dot(a, b):
    return jnp.dot(a, b, preferred_element_type=F32)


def _dot_nt(a, b):
    return lax.dot_general(a, b, (((1,), (1,)), ((), ())), preferred_element_type=F32)


def _dot_tn(a, b):
    return lax.dot_general(a, b, (((0,), (0,)), ((), ())), preferred_element_type=F32)


def _sigmoid(x):
    return 1.0 / (1.0 + jnp.exp(-x))


def _layer_norm(y, g, b):
    mu = jnp.mean(y, axis=-1, keepdims=True)
    yc = y - mu
    var = jnp.mean(yc * yc, axis=-1, keepdims=True)
    return yc * lax.rsqrt(var + LN_EPS) * g + b


class _Fillers:
    def __init__(self, thunks):
        self._thunks = list(thunks)

    def issue(self):
        if self._thunks:
            self._thunks.pop(0)()

    def flush(self):
        while self._thunks:
            self._thunks.pop(0)()


def _front(x, win_ref, vec_ref, tri_ref, proj_scr, q_scr, kk_scr, lf_scr, b_scr, conv_fn=None):
    xb = x.astype(BF16)
    proj_scr[:, OFF_B:PROJ_WIDTH] = _dot(xb, win_ref[:, OFF_B:PROJ_WIDTH])
    proj_scr[:, OFF_Q:OFF_I] = _dot(xb, win_ref[:, OFF_Q:OFF_I])
    if conv_fn is not None:
        conv_fn()
    proj_scr[:, OFF_I:OFF_B] = _dot(xb, win_ref[:, OFF_I:OFF_B])
    qp = proj_scr[:, OFF_Q:OFF_Q + HGRN_WIDTH]
    q_scr[...] = qp * _sigmoid(qp)
    z = proj_scr[:, OFF_F:OFF_F + HGRN_WIDTH]
    log_lb = vec_ref[ROW_LOGLB:ROW_LOGLB + 1, 0:HGRN_WIDTH]
    log_1mlb = vec_ref[ROW_LOGLB:ROW_LOGLB + 1, HGRN_WIDTH:2 * HGRN_WIDTH]
    one_m_lb = vec_ref[ROW_GATE:ROW_GATE + 1, 0:HGRN_WIDTH]
    e = jnp.exp(-jnp.abs(z))
    log_sig = jnp.minimum(z, 0.0) - jnp.log1p(e)
    c = log_1mlb + log_sig
    lf = LOG2_E * (jnp.maximum(log_lb, c) + jnp.log1p(jnp.exp(-jnp.abs(log_lb - c))))
    lf_scr[...] = lf
    kk_scr[...] = one_m_lb * jnp.where(z >= 0, e, 1.0) / (1.0 + e)
    hi = lf.astype(BF16)
    r1 = lf - hi.astype(F32)
    mid = r1.astype(BF16)
    lo = (r1 - mid.astype(F32)).astype(BF16)
    tri = tri_ref[...]
    b_scr[...] = _dot(tri, hi) + _dot(tri, mid) + _dot(tri, lo)


def _level_operand(h, half, tm, q_scr, kk_scr, lf_scr, b_scr):
    hs = slice(h * HEAD_DIM, (h + 1) * HEAD_DIM)
    if half >= 8:
        blk = 2 * half
        pieces = []
        for r0 in range(0, tm, blk):
            bm = b_scr[r0 + half - 1:r0 + half, hs]
            pieces.append(kk_scr[r0:r0 + half, hs] * jnp.exp2(bm - b_scr[r0:r0 + half, hs]))
            pieces.append(q_scr[r0 + half:r0 + blk, hs] * jnp.exp2(b_scr[r0 + half:r0 + blk, hs] - bm))
        return jnp.concatenate(pieces, axis=0).astype(BF16)
    row = lax.broadcasted_iota(jnp.int32, (tm, HEAD_DIM), 0)
    first = (row & half) == 0
    lfh = lf_scr[:, hs]
    if half == 4:
        bh = b_scr[:, hs]
        bm = jnp.concatenate([jnp.broadcast_to(b_scr[r0 + 3:r0 + 4, hs], (8, HEAD_DIM))
                              for r0 in range(0, tm, 8)], axis=0)
        arg = jnp.where(first, bm - bh, bh - bm)
    elif half == 2:
        r = row & 3
        lf_next = pltpu.roll(lfh, tm - 1, 0)
        lf_prev = pltpu.roll(lfh, 1, 0)
        arg = jnp.where(r == 0, lf_next, jnp.where(r == 1, 0.0, jnp.where(r == 2, lfh, lfh + lf_prev)))
    else:
        arg = jnp.where(first, 0.0, lfh)
    return (jnp.where(first, kk_scr[:, hs], q_scr[:, hs]) * jnp.exp2(arg)).astype(BF16)


def _intra_head(h, seg, tm, proj_scr, q_scr, kk_scr, lf_scr, b_scr, lv_ref, fill=None):
    fill = fill or _Fillers(())
    hs = slice(h * HEAD_DIM, (h + 1) * HEAD_DIM)
    vh = proj_scr[:, OFF_I + h * HEAD_DIM:OFF_I + (h + 1) * HEAD_DIM]
    n_diag = tm // DIAG
    group = lv_ref.shape[1] // DIAG
    lv = lv_ref[...]
    diag = [None] * (n_diag // group)
    cross = {}
    zero = jnp.zeros((DIAG, HEAD_DIM), BF16)
    half = seg // 2
    while half >= 1:
        xl = _level_operand(h, half, tm, q_scr, kk_scr, lf_scr, b_scr)
        if half >= DIAG:
            for r0 in range(0, tm, 2 * half):
                for tq in range(r0 + half, r0 + 2 * half, DIAG):
                    for tk in range(r0, r0 + half, DIAG):
                        cross[(tq // DIAG, tk // DIAG)] = _dot_nt(xl[tq:tq + DIAG], xl[tk:tk + DIAG])
        else:
            lg = int(math.log2(half))
            for g in range(n_diag // group):
                xs = [xl[(g * group + d) * DIAG:(g * group + d + 1) * DIAG] for d in range(group)]
                if group == 1:
                    p = _dot_nt(xs[0], xs[0])
                else:
                    rhs = jnp.concatenate(
                        [jnp.concatenate([xs[d] if e == d else zero for e in range(group)], axis=1)
                         for d in range(group)], axis=0)
                    p = _dot_nt(jnp.concatenate(xs, axis=1), rhs)
                diag[g] = jnp.where(lv == lg, p, 0.0 if diag[g] is None else diag[g])
        if int(math.log2(half)) % FILL_EVERY == 0:
            fill.issue()
        half //= 2
    vb = vh.astype(BF16)
    outs = []
    for d in range(n_diag):
        own = diag[d // group][:, (d % group) * DIAG:(d % group + 1) * DIAG]
        p = jnp.concatenate([cross[(d, e)] for e in range(d)] + [own], axis=1).astype(BF16)
        outs.append(_dot(p, vb[0:(d + 1) * DIAG]))
    dg = jnp.sum(q_scr[:, hs] * kk_scr[:, hs], axis=-1, keepdims=True)
    return jnp.concatenate(outs, axis=0) + dg * vh


def _finish_head(h, o, proj_scr, vec_ref, o_scr):
    hs = slice(h * HEAD_DIM, (h + 1) * HEAD_DIM)
    g = proj_scr[:, OFF_G + h * HEAD_DIM:OFF_G + (h + 1) * HEAD_DIM]
    onorm = vec_ref[ROW_GATE:ROW_GATE + 1, HGRN_WIDTH + h * HEAD_DIM:HGRN_WIDTH + (h + 1) * HEAD_DIM]
    o = o * lax.rsqrt(jnp.mean(o * o, axis=-1, keepdims=True) + RMS_EPS)
    o_scr[:, hs] = o * onorm * (g * _sigmoid(g))


def _mix_norm(x, mixin, wout_ref, vec_ref):
    mix = _dot(mixin, wout_ref[...])
    return _layer_norm(ALPHA * x + mix, vec_ref[ROW_LN1G:ROW_LN1G + 1, :], vec_ref[ROW_LN1B:ROW_LN1B + 1, :])


def _ffn_chunk(c, x1b, w1_ref, w2_ref):
    cs = slice(c * FF_CHUNK, (c + 1) * FF_CHUNK)
    hid = jnp.maximum(_dot(x1b, w1_ref[:, cs]), 0.0)
    return _dot((hid * hid).astype(BF16), w2_ref[cs, :])


def _ffn_norm(x1, ffn, vec_ref):
    return _layer_norm(ALPHA * x1 + ffn, vec_ref[ROW_LN2G:ROW_LN2G + 1, :], vec_ref[ROW_LN2B:ROW_LN2B + 1, :])


def _back(x, mixin, wout_ref, w1_ref, w2_ref, vec_ref):
    x1 = _mix_norm(x, mixin, wout_ref, vec_ref)
    x1b = x1.astype(BF16)
    acc = _ffn_chunk(0, x1b, w1_ref, w2_ref)
    for c in range(1, N_FF_CHUNKS):
        acc = acc + _ffn_chunk(c, x1b, w1_ref, w2_ref)
    return _ffn_norm(x1, acc, vec_ref)


def _conv_weights(vec_ref):
    w0 = vec_ref[ROW_CW01:ROW_CW01 + 1, 0:CONV_WIDTH]
    w1 = vec_ref[ROW_CW01:ROW_CW01 + 1, CONV_WIDTH:2 * CONV_WIDTH]
    w2 = vec_ref[ROW_CW2:ROW_CW2 + 1, 0:CONV_WIDTH]
    return w0, w1, w2


def _prompt_kernel(x_ref, win_ref, wout_ref, w1_ref, w2_ref, vec_ref, lv_ref, tri_ref,
                   y_ref, hst_ref, cst_ref,
                   proj_scr, q_scr, kk_scr, lf_scr, b_scr, u_scr, o_scr, xs_scr, mix_scr,
                   x1_scr, x1b_scr, acc_scr, hid_scr, *, n_tiles, tiles_per_seq):
    tm = PROMPT_TILE
    i = pl.program_id(0)
    live = i < n_tiles
    j = jnp.minimum(i, n_tiles - 1) % tiles_per_seq

    @pl.when(i == 0)
    def _():
        xs_scr[...] = jnp.zeros(xs_scr.shape, F32)
        mix_scr[...] = jnp.zeros(mix_scr.shape, BF16)

    @pl.when(jnp.logical_and(j == 0, live))
    def _():
        hst_ref[...] = jnp.zeros(hst_ref.shape, F32)
        u_scr[0:8, :] = jnp.zeros((8, CONV_WIDTH), F32)

    x1 = _mix_norm(xs_scr[...], mix_scr[...], wout_ref, vec_ref)
    x1_scr[...] = x1
    x1b_scr[...] = x1.astype(BF16)

    x = x_ref[0]

    def conv_fn():
        u = proj_scr[:, OFF_C:OFF_C + CONV_WIDTH] * proj_scr[:, OFF_H:OFF_H + CONV_WIDTH]
        u_scr[8:8 + tm, :] = u
        w0, w1, w2 = _conv_weights(vec_ref)
        conv = w0 * u_scr[6:6 + tm, :] + w1 * u_scr[7:7 + tm, :] + w2 * u
        mix_scr[:, HGRN_WIDTH:HGRN_WIDTH + CONV_WIDTH] = (proj_scr[:, OFF_B:OFF_B + CONV_WIDTH] * conv).astype(BF16)
        tail = u_scr[tm:tm + 8, :]
        u_scr[0:8, :] = tail
        cst_ref[0] = tail[6:8, :]

    def ffn_fillers(c):
        def up(j):
            cols = slice(c * FF_CHUNK + j * FF_SUB, c * FF_CHUNK + (j + 1) * FF_SUB)
            hid = jnp.maximum(_dot(x1b_scr[...], w1_ref[:, cols]), 0.0)
            hid_scr[:, j * FF_SUB:(j + 1) * FF_SUB] = (hid * hid).astype(BF16)

        def down(j):
            cols = slice(j * FF_SUB, (j + 1) * FF_SUB)
            part = _dot(hid_scr[...], w2_ref[c * FF_CHUNK:(c + 1) * FF_CHUNK, cols])
            acc_scr[:, cols] = part if c == 0 else acc_scr[:, cols] + part

        return ([functools.partial(up, j) for j in range(FF_CHUNK // FF_SUB)]
                + [functools.partial(down, j) for j in range(D_MODEL // FF_SUB)])

    fill = _Fillers([f for c in range(N_FF_CHUNKS) for f in ffn_fillers(c)])
    _front(x, win_ref, vec_ref, tri_ref, proj_scr, q_scr, kk_scr, lf_scr, b_scr, conv_fn)

    for h in range(N_HEADS):
        hs = slice(h * HEAD_DIM, (h + 1) * HEAD_DIM)
        o = _intra_head(h, tm, tm, proj_scr, q_scr, kk_scr, lf_scr, b_scr, lv_ref, fill)
        qh = q_scr[:, hs]
        kh = kk_scr[:, hs]
        bh = b_scr[:, hs]
        vh = proj_scr[:, OFF_I + h * HEAD_DIM:OFF_I + (h + 1) * HEAD_DIM]
        s_prev = hst_ref[0, h]
        o = o + _dot((qh * jnp.exp2(bh)).astype(BF16), s_prev.astype(BF16))
        b_last = b_scr[tm - 1:tm, hs]
        kd = (kh * jnp.exp2(b_last - bh)).astype(BF16)
        upd = _dot_tn(kd, vh.astype(BF16))
        a_col = jnp.exp2(jnp.transpose(b_scr[tm - 8:tm, hs])[:, 7:8])
        hst_ref[0, h] = jnp.where(live, a_col * s_prev + upd, s_prev)
        _finish_head(h, o, proj_scr, vec_ref, o_scr)

    fill.flush()
    y_ref[0] = _ffn_norm(x1_scr[...], acc_scr[...], vec_ref)
    xs_scr[...] = x
    mix_scr[:, 0:HGRN_WIDTH] = o_scr[...].astype(BF16)


def _sample_kernel(x_ref, sin_ref, ext_ref, win_ref, wout_ref, w1_ref, w2_ref, vec_ref, lv_ref, tri_ref,
                   y_ref, sout_ref, u_ref,
                   proj_scr, q_scr, kk_scr, lf_scr, b_scr, o_scr, qe_scr, kd_scr, vb_scr, ys_scr):
    seg = 8
    tm = SAMPLE_SEQS * seg
    rows = pl.ds(pl.multiple_of(pl.program_id(1) * tm, tm), tm)

    @pl.when(pl.program_id(0) == 0)
    def _():
        ys_scr[rows, :] = x_ref[...]

    x = ys_scr[rows, :]
    _front(x, win_ref, vec_ref, tri_ref, proj_scr, q_scr, kk_scr, lf_scr, b_scr)

    for h in range(N_HEADS):
        hs = slice(h * HEAD_DIM, (h + 1) * HEAD_DIM)
        o_scr[:, hs] = _intra_head(h, seg, tm, proj_scr, q_scr, kk_scr, lf_scr, b_scr, lv_ref)
        bh = b_scr[:, hs]
        pieces = [jnp.broadcast_to(b_scr[pl.ds(seg * jb + seg - 1, 1), hs], (seg, HEAD_DIM))
                  for jb in range(tm // seg)]
        b_last = jnp.concatenate(pieces, axis=0)
        qe_scr[:, hs] = (q_scr[:, hs] * jnp.exp2(bh)).astype(BF16)
        kd_scr[:, hs] = (kk_scr[:, hs] * jnp.exp2(b_last - bh)).astype(BF16)
    vb_scr[...] = proj_scr[:, OFF_I:OFF_I + HGRN_WIDTH].astype(BF16)

    top = lax.broadcasted_iota(jnp.int32, (2 * seg, HEAD_DIM), 0) < seg
    for h in range(N_HEADS):
        hs = slice(h * HEAD_DIM, (h + 1) * HEAD_DIM)
        b_last = jnp.concatenate([b_scr[r:r + 1, hs] for r in range(seg - 1, tm, seg)], axis=0)
        a_cols = jnp.exp2(jnp.transpose(b_last))
        for p in range(SAMPLE_SEQS // 2):
            rows16 = slice(p * 2 * seg, (p + 1) * 2 * seg)
            qe = qe_scr[rows16, hs]
            kd = kd_scr[rows16, hs]
            vv = vb_scr[rows16, hs]
            s_a = sin_ref[2 * p, h]
            s_b = sin_ref[2 * p + 1, h]
            o_ab = _dot(qe, jnp.concatenate([s_a, s_b], axis=1).astype(BF16))
            o_scr[rows16, hs] = o_scr[rows16, hs] + jnp.where(top, o_ab[:, 0:HEAD_DIM], o_ab[:, HEAD_DIM:2 * HEAD_DIM])
            zero = jnp.zeros_like(kd)
            upd = _dot_tn(jnp.concatenate([jnp.where(top, kd, zero), jnp.where(top, zero, kd)], axis=1), vv)
            sout_ref[2 * p, h] = a_cols[:, 2 * p:2 * p + 1] * s_a + upd[0:HEAD_DIM]
            sout_ref[2 * p + 1, h] = a_cols[:, 2 * p + 1:2 * p + 2] * s_b + upd[HEAD_DIM:2 * HEAD_DIM]

    for h in range(N_HEADS):
        hs = slice(h * HEAD_DIM, (h + 1) * HEAD_DIM)
        _finish_head(h, o_scr[:, hs], proj_scr, vec_ref, o_scr)

    u = proj_scr[:, OFF_C:OFF_C + CONV_WIDTH] * proj_scr[:, OFF_H:OFF_H + CONV_WIDTH]
    u_ref[...] = u
    ext = ext_ref[...]
    tmod = lax.broadcasted_iota(jnp.int32, (tm, CONV_WIDTH), 0) & (seg - 1)
    u1 = jnp.where(tmod == 0, pltpu.roll(ext, tm - 1, 0), pltpu.roll(u, 1, 0))
    u2 = jnp.where(tmod < 2, ext, pltpu.roll(u, 2, 0))
    w0, w1, w2 = _conv_weights(vec_ref)
    yc = proj_scr[:, OFF_B:OFF_B + CONV_WIDTH] * (w0 * u2 + w1 * u1 + w2 * u)

    mixin = jnp.concatenate([o_scr[...], yc], axis=-1).astype(BF16)
    y = _back(x, mixin, wout_ref, w1_ref, w2_ref, vec_ref)
    ys_scr[rows, :] = y
    y_ref[...] = y


def _level_table(tm, seg):
    t = np.arange(tm)[:, None]
    s = np.arange(tm)[None, :]
    x = np.maximum(t ^ s, 1)
    lv = np.floor(np.log2(x)).astype(np.int32)
    ok = (s < t) & ((t // seg) == (s // seg))
    return np.where(ok, lv, -1).astype(np.int32)


def _tri_table(tm, seg):
    t = np.arange(tm)[:, None]
    s = np.arange(tm)[None, :]
    return ((s <= t) & ((t // seg) == (s // seg))).astype(np.float32)


def _const_spec(shape):
    nd = len(shape)
    return pl.BlockSpec(shape, lambda *_: (0,) * nd, pipeline_mode=pl.Buffered(1))


def _weight_specs(layer_of):
    def wspec(k, n):
        return pl.BlockSpec((None, k, n), lambda *g: (layer_of(*g), 0, 0), pipeline_mode=pl.Buffered(1))
    return [wspec(D_MODEL, PROJ_WIDTH), wspec(D_MODEL, D_MODEL), wspec(D_MODEL, D_FF), wspec(D_FF, D_MODEL),
            wspec(8, D_MODEL)]


def _prompt_layer(layer, x, weights, vecs):
    bsz, seq, _ = x.shape
    tm = PROMPT_TILE
    tiles_per_seq = seq // tm
    n_tiles = bsz * tiles_per_seq
    lv = jnp.asarray(np.tile(_level_table(DIAG, DIAG), (1, SCORE_GROUP)))
    tri = jnp.asarray(_tri_table(tm, tm), dtype=BF16)

    def cur(i):
        t = jnp.minimum(i, n_tiles - 1)
        return t // tiles_per_seq, t % tiles_per_seq

    def prev(i):
        t = jnp.maximum(i - 1, 0)
        return t // tiles_per_seq, t % tiles_per_seq

    return pl.pallas_call(
        functools.partial(_prompt_kernel, n_tiles=n_tiles, tiles_per_seq=tiles_per_seq),
        grid=(n_tiles + 1,),
        in_specs=[pl.BlockSpec((1, tm, D_MODEL), lambda i: (*cur(i), 0))] + _weight_specs(lambda i: layer)
        + [_const_spec((DIAG, SCORE_GROUP * DIAG)), _const_spec((tm, tm))],
        out_specs=[pl.BlockSpec((1, tm, D_MODEL), lambda i: (*prev(i), 0)),
                   pl.BlockSpec((1, N_HEADS, HEAD_DIM, HEAD_DIM), lambda i: (cur(i)[0], 0, 0, 0)),
                   pl.BlockSpec((1, CONV_K - 1, CONV_WIDTH), lambda i: (cur(i)[0], 0, 0))],
        out_shape=[jax.ShapeDtypeStruct((bsz, seq, D_MODEL), F32),
                   jax.ShapeDtypeStruct((bsz, N_HEADS, HEAD_DIM, HEAD_DIM), F32),
                   jax.ShapeDtypeStruct((bsz, CONV_K - 1, CONV_WIDTH), F32)],
        scratch_shapes=[pltpu.VMEM((tm, PROJ_WIDTH), F32),
                        pltpu.VMEM((tm, HGRN_WIDTH), F32), pltpu.VMEM((tm, HGRN_WIDTH), F32),
                        pltpu.VMEM((tm, HGRN_WIDTH), F32), pltpu.VMEM((tm, HGRN_WIDTH), F32),
                        pltpu.VMEM((tm + 8, CONV_WIDTH), F32), pltpu.VMEM((tm, HGRN_WIDTH), F32),
                        pltpu.VMEM((tm, D_MODEL), F32), pltpu.VMEM((tm, D_MODEL), BF16),
                        pltpu.VMEM((tm, D_MODEL), F32), pltpu.VMEM((tm, D_MODEL), BF16),
                        pltpu.VMEM((tm, D_MODEL), F32), pltpu.VMEM((tm, FF_CHUNK), BF16)],
        compiler_params=pltpu.CompilerParams(dimension_semantics=("arbitrary",),
                                             vmem_limit_bytes=V7X_VMEM_LIMIT),
        name=f"prompt_layer{layer}",
    )(x, *weights, vecs, lv, tri)


def _sample_layers(x, state, ext, weights, vecs):
    rows = x.shape[0]
    seg = 8
    tm = SAMPLE_SEQS * seg
    n_seq = rows // seg
    n_steps = n_seq // SAMPLE_SEQS
    lv = jnp.asarray(_level_table(DIAG, seg))
    tri = jnp.asarray(_tri_table(tm, seg), dtype=BF16)
    st_spec = pl.BlockSpec((None, SAMPLE_SEQS, N_HEADS, HEAD_DIM, HEAD_DIM), lambda l, i: (l, i, 0, 0, 0))
    u_spec = pl.BlockSpec((None, tm, CONV_WIDTH), lambda l, i: (l, i, 0))
    return pl.pallas_call(
        _sample_kernel,
        grid=(DEPTH, n_steps),
        in_specs=[pl.BlockSpec((tm, D_MODEL), lambda l, i: (i * (1 - l) + (n_steps - 1) * l, 0)), st_spec, u_spec]
        + _weight_specs(lambda l, i: l) + [_const_spec((DIAG, DIAG)), _const_spec((tm, tm))],
        out_specs=[pl.BlockSpec((tm, D_MODEL), lambda l, i: (i * l, 0)), st_spec, u_spec],
        out_shape=[jax.ShapeDtypeStruct((rows, D_MODEL), F32),
                   jax.ShapeDtypeStruct((DEPTH, n_seq, N_HEADS, HEAD_DIM, HEAD_DIM), F32),
                   jax.ShapeDtypeStruct((DEPTH, rows, CONV_WIDTH), F32)],
        scratch_shapes=[pltpu.VMEM((tm, PROJ_WIDTH), F32),
                        pltpu.VMEM((tm, HGRN_WIDTH), F32), pltpu.VMEM((tm, HGRN_WIDTH), F32),
                        pltpu.VMEM((tm, HGRN_WIDTH), F32), pltpu.VMEM((tm, HGRN_WIDTH), F32),
                        pltpu.VMEM((tm, HGRN_WIDTH), F32),
                        pltpu.VMEM((tm, HGRN_WIDTH), BF16), pltpu.VMEM((tm, HGRN_WIDTH), BF16),
                        pltpu.VMEM((tm, HGRN_WIDTH), BF16), pltpu.VMEM((rows, D_MODEL), F32)],
        compiler_params=pltpu.CompilerParams(dimension_semantics=("arbitrary", "arbitrary"),
                                             vmem_limit_bytes=V7X_VMEM_LIMIT),
        name="sample_layers",
    )(x, state, ext, *weights, vecs, lv, tri)


def kernel(x_prompt, x_sample, state_hgrn, state_conv, w_in, lb_logits, conv_w, onorm_g, w_out,
           ln1_g, ln1_b, w_ff1, w_ff2, ln2_g, ln2_b):
    n_seq, dec_seq, _ = x_sample.shape
    p = jax.nn.softmax(lb_logits.astype(F32), axis=0)
    cum = jnp.cumsum(p, axis=0)
    lb = cum - cum[0:1]
    zeros_half = jnp.zeros((DEPTH, CONV_WIDTH), F32)
    vecs = jnp.stack([
        ln1_g, ln1_b, ln2_g, ln2_b,
        jnp.concatenate([jnp.log(lb), jnp.log1p(-lb)], axis=-1),
        jnp.concatenate([1.0 - lb, onorm_g], axis=-1),
        jnp.concatenate([conv_w[:, 0], conv_w[:, 1]], axis=-1),
        jnp.concatenate([conv_w[:, 2], zeros_half], axis=-1)], axis=1).astype(F32)
    weights = (w_in.astype(BF16), w_out.astype(BF16), w_ff1.astype(BF16), w_ff2.astype(BF16))
    ext = jnp.pad(state_conv, ((0, 0), (0, 0), (0, dec_seq - (CONV_K - 1)), (0, 0)))
    ext = ext.reshape(DEPTH, n_seq * dec_seq, CONV_WIDTH)

    yp = x_prompt
    ys = x_sample.reshape(n_seq * dec_seq, D_MODEL)
    hp, cp = [], []
    for layer in range(DEPTH):
        yp, hgrn_p, conv_p = _prompt_layer(layer, yp, weights, vecs)
        hp.append(hgrn_p)
        cp.append(conv_p)
    ys, hgrn_s, u_s = _sample_layers(ys, state_hgrn, ext, weights, vecs)
    conv_s = u_s.reshape(DEPTH, n_seq, dec_seq, CONV_WIDTH)[:, :, dec_seq - (CONV_K - 1):]
    return (yp, ys.reshape(n_seq, dec_seq, D_MODEL), jnp.stack(hp), jnp.stack(cp), hgrn_s, conv_s)
```

```python
import functools
import math

import numpy as np
import jax
import jax.numpy as jnp
from jax import lax
from jax.experimental import pallas as pl
from jax.experimental.pallas import tpu as pltpu

F32 = jnp.float32
BF16 = jnp.bfloat16

D_MODEL = 1024
DEPTH = 2
HGRN_WIDTH = 512
CONV_WIDTH = 512
HEAD_DIM = 128
N_HEADS = HGRN_WIDTH // HEAD_DIM
CONV_K = 3
D_FF = 4 * D_MODEL
PROJ_WIDTH = 4 * HGRN_WIDTH + 3 * CONV_WIDTH
FF_CHUNK = 1024
N_FF_CHUNKS = D_FF // FF_CHUNK
ALPHA = float((2 * DEPTH) ** 0.25)
LN_EPS = 1e-5
RMS_EPS = 1e-6

OFF_Q, OFF_F, OFF_I, OFF_G = 0, HGRN_WIDTH, 2 * HGRN_WIDTH, 3 * HGRN_WIDTH
OFF_B = 4 * HGRN_WIDTH
OFF_C = OFF_B + CONV_WIDTH
OFF_H = OFF_C + CONV_WIDTH

LOG2_E = math.log2(math.e)
DIAG = 128
SCORE_GROUP = 2
FF_SUB = 256
PROMPT_TILE = 256
SAMPLE_SEQS = 16
V7X_VMEM_LIMIT = 58 * 1024 * 1024

ROW_LN1G, ROW_LN1B, ROW_LN2G, ROW_LN2B, ROW_LOGLB, ROW_GATE, ROW_CW01, ROW_CW2 = range(8)


def _dot(a, b):
    return jnp.dot(a, b, preferred_element_type=F32)


def _dot_nt(a, b):
    return lax.dot_general(a, b, (((1,), (1,)), ((), ())), preferred_element_type=F32)


def _dot_tn(a, b):
    return lax.dot_general(a, b, (((0,), (0,)), ((), ())), preferred_element_type=F32)


def _sigmoid(x):
    return 1.0 / (1.0 + jnp.exp(-x))


def _layer_norm(y, g, b):
    mu = jnp.mean(y, axis=-1, keepdims=True)
    yc = y - mu
    var = jnp.mean(yc * yc, axis=-1, keepdims=True)
    return yc * lax.rsqrt(var + LN_EPS) * g + b


class _Fillers:
    def __init__(self, thunks):
        self._thunks = list(thunks)

    def issue(self):
        if self._thunks:
            self._thunks.pop(0)()

    def flush(self):
        while self._thunks:
            self._thunks.pop(0)()


def _front(x, win_ref, vec_ref, tri_ref, proj_scr, q_scr, kk_scr, lf_scr, b_scr, conv_fn=None):
    xb = x.astype(BF16)
    proj_scr[:, OFF_B:PROJ_WIDTH] = _dot(xb, win_ref[:, OFF_B:PROJ_WIDTH])
    proj_scr[:, OFF_Q:OFF_I] = _dot(xb, win_ref[:, OFF_Q:OFF_I])
    if conv_fn is not None:
        conv_fn()
    proj_scr[:, OFF_I:OFF_B] = _dot(xb, win_ref[:, OFF_I:OFF_B])
    qp = proj_scr[:, OFF_Q:OFF_Q + HGRN_WIDTH]
    q_scr[...] = qp * _sigmoid(qp)
    z = proj_scr[:, OFF_F:OFF_F + HGRN_WIDTH]
    log_lb = vec_ref[ROW_LOGLB:ROW_LOGLB + 1, 0:HGRN_WIDTH]
    log_1mlb = vec_ref[ROW_LOGLB:ROW_LOGLB + 1, HGRN_WIDTH:2 * HGRN_WIDTH]
    one_m_lb = vec_ref[ROW_GATE:ROW_GATE + 1, 0:HGRN_WIDTH]
    e = jnp.exp(-jnp.abs(z))
    log_sig = jnp.minimum(z, 0.0) - jnp.log1p(e)
    c = log_1mlb + log_sig
    lf = LOG2_E * (jnp.maximum(log_lb, c) + jnp.log1p(jnp.exp(-jnp.abs(log_lb - c))))
    lf_scr[...] = lf
    kk_scr[...] = one_m_lb * jnp.where(z >= 0, e, 1.0) / (1.0 + e)
    hi = lf.astype(BF16)
    r1 = lf - hi.astype(F32)
    mid = r1.astype(BF16)
    lo = (r1 - mid.astype(F32)).astype(BF16)
    tri = tri_ref[...]
    b_scr[...] = _dot(tri, hi) + _dot(tri, mid) + _dot(tri, lo)


def _level_operand(h, half, tm, q_scr, kk_scr, lf_scr, b_scr):
    hs = slice(h * HEAD_DIM, (h + 1) * HEAD_DIM)
    if half >= 8:
        blk = 2 * half
        pieces = []
        for r0 in range(0, tm, blk):
            bm = b_scr[r0 + half - 1:r0 + half, hs]
            pieces.append(kk_scr[r0:r0 + half, hs] * jnp.exp2(bm - b_scr[r0:r0 + half, hs]))
            pieces.append(q_scr[r0 + half:r0 + blk, hs] * jnp.exp2(b_scr[r0 + half:r0 + blk, hs] - bm))
        return jnp.concatenate(pieces, axis=0).astype(BF16)
    row = lax.broadcasted_iota(jnp.int32, (tm, HEAD_DIM), 0)
    first = (row & half) == 0
    lfh = lf_scr[:, hs]
    if half == 4:
        bh = b_scr[:, hs]
        bm = jnp.concatenate([jnp.broadcast_to(b_scr[r0 + 3:r0 + 4, hs], (8, HEAD_DIM))
                              for r0 in range(0, tm, 8)], axis=0)
        arg = jnp.where(first, bm - bh, bh - bm)
    elif half == 2:
        r = row & 3
        lf_next = pltpu.roll(lfh, tm - 1, 0)
        lf_prev = pltpu.roll(lfh, 1, 0)
        arg = jnp.where(r == 0, lf_next, jnp.where(r == 1, 0.0, jnp.where(r == 2, lfh, lfh + lf_prev)))
    else:
        arg = jnp.where(first, 0.0, lfh)
    return (jnp.where(first, kk_scr[:, hs], q_scr[:, hs]) * jnp.exp2(arg)).astype(BF16)


def _intra_head(h, seg, tm, proj_scr, q_scr, kk_scr, lf_scr, b_scr, lv_ref, fill=None):
    fill = fill or _Fillers(())
    hs = slice(h * HEAD_DIM, (h + 1) * HEAD_DIM)
    vh = proj_scr[:, OFF_I + h * HEAD_DIM:OFF_I + (h + 1) * HEAD_DIM]
    n_diag = tm // DIAG
    group = lv_ref.shape[1] // DIAG
    lv = lv_ref[...]
    diag = [None] * (n_diag // group)
    cross = {}
    zero = jnp.zeros((DIAG, HEAD_DIM), BF16)
    half = seg // 2
    while half >= 1:
        xl = _level_operand(h, half, tm, q_scr, kk_scr, lf_scr, b_scr)
        if half >= DIAG:
            for r0 in range(0, tm, 2 * half):
                for tq in range(r0 + half, r0 + 2 * half, DIAG):
                    for tk in range(r0, r0 + half, DIAG):
                        cross[(tq // DIAG, tk // DIAG)] = _dot_nt(xl[tq:tq + DIAG], xl[tk:tk + DIAG])
        else:
            lg = int(math.log2(half))
            for g in range(n_diag // group):
                xs = [xl[(g * group + d) * DIAG:(g * group + d + 1) * DIAG] for d in range(group)]
                if group == 1:
                    p = _dot_nt(xs[0], xs[0])
                else:
                    rhs = jnp.concatenate(
                        [jnp.concatenate([xs[d] if e == d else zero for e in range(group)], axis=1)
                         for d in range(group)], axis=0)
                    p = _dot_nt(jnp.concatenate(xs, axis=1), rhs)
                diag[g] = jnp.where(lv == lg, p, 0.0 if diag[g] is None else diag[g])
        fill.issue()
        half //= 2
    vb = vh.astype(BF16)
    outs = []
    for d in range(n_diag):
        own = diag[d // group][:, (d % group) * DIAG:(d % group + 1) * DIAG]
        p = jnp.concatenate([cross[(d, e)] for e in range(d)] + [own], axis=1).astype(BF16)
        outs.append(_dot(p, vb[0:(d + 1) * DIAG]))
    dg = jnp.sum(q_scr[:, hs] * kk_scr[:, hs], axis=-1, keepdims=True)
    return jnp.concatenate(outs, axis=0) + dg * vh


def _intra_head_pair(h, seg, proj_scr, q_scr, kk_scr, lf_scr, b_scr, lv_ref):
    tm = DIAG
    lv = lv_ref[...]
    zero = jnp.zeros((DIAG, HEAD_DIM), BF16)
    acc = None
    half = seg // 2
    while half >= 1:
        xa = _level_operand(h, half, tm, q_scr, kk_scr, lf_scr, b_scr)
        xb = _level_operand(h + 1, half, tm, q_scr, kk_scr, lf_scr, b_scr)
        rhs = jnp.concatenate([jnp.concatenate([xa, zero], axis=1), jnp.concatenate([zero, xb], axis=1)], axis=0)
        p = _dot_nt(jnp.concatenate([xa, xb], axis=1), rhs)
        acc = jnp.where(lv == int(math.log2(half)), p, 0.0 if acc is None else acc)
        half //= 2
    outs = []
    for j in range(2):
        hs = slice((h + j) * HEAD_DIM, (h + j + 1) * HEAD_DIM)
        vh = proj_scr[:, OFF_I + (h + j) * HEAD_DIM:OFF_I + (h + j + 1) * HEAD_DIM]
        dg = jnp.sum(q_scr[:, hs] * kk_scr[:, hs], axis=-1, keepdims=True)
        outs.append(_dot(acc[:, j * DIAG:(j + 1) * DIAG].astype(BF16), vh.astype(BF16)) + dg * vh)
    return outs


def _finish_head(h, o, proj_scr, vec_ref, o_scr):
    hs = slice(h * HEAD_DIM, (h + 1) * HEAD_DIM)
    g = proj_scr[:, OFF_G + h * HEAD_DIM:OFF_G + (h + 1) * HEAD_DIM]
    onorm = vec_ref[ROW_GATE:ROW_GATE + 1, HGRN_WIDTH + h * HEAD_DIM:HGRN_WIDTH + (h + 1) * HEAD_DIM]
    o = o * lax.rsqrt(jnp.mean(o * o, axis=-1, keepdims=True) + RMS_EPS)
    o_scr[:, hs] = o * onorm * (g * _sigmoid(g))


def _mix_norm(x, mixin, wout_ref, vec_ref):
    mix = _dot(mixin, wout_ref[...])
    return _layer_norm(ALPHA * x + mix, vec_ref[ROW_LN1G:ROW_LN1G + 1, :], vec_ref[ROW_LN1B:ROW_LN1B + 1, :])


def _ffn_chunk(c, x1b, w1_ref, w2_ref):
    cs = slice(c * FF_CHUNK, (c + 1) * FF_CHUNK)
    hid = jnp.maximum(_dot(x1b, w1_ref[:, cs]), 0.0)
    return _dot((hid * hid).astype(BF16), w2_ref[cs, :])


def _ffn_norm(x1, ffn, vec_ref):
    return _layer_norm(ALPHA * x1 + ffn, vec_ref[ROW_LN2G:ROW_LN2G + 1, :], vec_ref[ROW_LN2B:ROW_LN2B + 1, :])


def _back(x, mixin, wout_ref, w1_ref, w2_ref, vec_ref):
    x1 = _mix_norm(x, mixin, wout_ref, vec_ref)
    x1b = x1.astype(BF16)
    acc = _ffn_chunk(0, x1b, w1_ref, w2_ref)
    for c in range(1, N_FF_CHUNKS):
        acc = acc + _ffn_chunk(c, x1b, w1_ref, w2_ref)
    return _ffn_norm(x1, acc, vec_ref)


def _conv_weights(vec_ref):
    w0 = vec_ref[ROW_CW01:ROW_CW01 + 1, 0:CONV_WIDTH]
    w1 = vec_ref[ROW_CW01:ROW_CW01 + 1, CONV_WIDTH:2 * CONV_WIDTH]
    w2 = vec_ref[ROW_CW2:ROW_CW2 + 1, 0:CONV_WIDTH]
    return w0, w1, w2


def _prompt_kernel(x_ref, win_ref, wout_ref, w1_ref, w2_ref, vec_ref, lv_ref, tri_ref,
                   y_ref, hst_ref, cst_ref,
                   proj_scr, q_scr, kk_scr, lf_scr, b_scr, u_scr, o_scr, xs_scr, mix_scr,
                   x1_scr, x1b_scr, acc_scr, hid_scr, *, n_tiles, tiles_per_seq):
    tm = PROMPT_TILE
    i = pl.program_id(0)
    live = i < n_tiles
    j = jnp.minimum(i, n_tiles - 1) % tiles_per_seq

    @pl.when(i == 0)
    def _():
        xs_scr[...] = jnp.zeros(xs_scr.shape, F32)
        mix_scr[...] = jnp.zeros(mix_scr.shape, BF16)

    @pl.when(jnp.logical_and(j == 0, live))
    def _():
        hst_ref[...] = jnp.zeros(hst_ref.shape, F32)
        u_scr[0:8, :] = jnp.zeros((8, CONV_WIDTH), F32)

    x1 = _mix_norm(xs_scr[...], mix_scr[...], wout_ref, vec_ref)
    x1_scr[...] = x1
    x1b_scr[...] = x1.astype(BF16)

    x = x_ref[0]

    def conv_fn():
        u = proj_scr[:, OFF_C:OFF_C + CONV_WIDTH] * proj_scr[:, OFF_H:OFF_H + CONV_WIDTH]
        u_scr[8:8 + tm, :] = u
        w0, w1, w2 = _conv_weights(vec_ref)
        conv = w0 * u_scr[6:6 + tm, :] + w1 * u_scr[7:7 + tm, :] + w2 * u
        mix_scr[:, HGRN_WIDTH:HGRN_WIDTH + CONV_WIDTH] = (proj_scr[:, OFF_B:OFF_B + CONV_WIDTH] * conv).astype(BF16)
        tail = u_scr[tm:tm + 8, :]
        u_scr[0:8, :] = tail
        cst_ref[0] = tail[6:8, :]

    def ffn_fillers(c):
        def up(j):
            cols = slice(c * FF_CHUNK + j * FF_SUB, c * FF_CHUNK + (j + 1) * FF_SUB)
            hid = jnp.maximum(_dot(x1b_scr[...], w1_ref[:, cols]), 0.0)
            hid_scr[:, j * FF_SUB:(j + 1) * FF_SUB] = (hid * hid).astype(BF16)

        def down(j):
            cols = slice(j * FF_SUB, (j + 1) * FF_SUB)
            part = _dot(hid_scr[...], w2_ref[c * FF_CHUNK:(c + 1) * FF_CHUNK, cols])
            acc_scr[:, cols] = part if c == 0 else acc_scr[:, cols] + part

        return ([functools.partial(up, j) for j in range(FF_CHUNK // FF_SUB)]
                + [functools.partial(down, j) for j in range(D_MODEL // FF_SUB)])

    fill = _Fillers([f for c in range(N_FF_CHUNKS) for f in ffn_fillers(c)])
    _front(x, win_ref, vec_ref, tri_ref, proj_scr, q_scr, kk_scr, lf_scr, b_scr, conv_fn)

    for h in range(N_HEADS):
        hs = slice(h * HEAD_DIM, (h + 1) * HEAD_DIM)
        o = _intra_head(h, tm, tm, proj_scr, q_scr, kk_scr, lf_scr, b_scr, lv_ref, fill)
        qh = q_scr[:, hs]
        kh = kk_scr[:, hs]
        bh = b_scr[:, hs]
        vh = proj_scr[:, OFF_I + h * HEAD_DIM:OFF_I + (h + 1) * HEAD_DIM]
        s_prev = hst_ref[0, h]
        o = o + _dot((qh * jnp.exp2(bh)).astype(BF16), s_prev.astype(BF16))
        b_last = b_scr[tm - 1:tm, hs]
        kd = (kh * jnp.exp2(b_last - bh)).astype(BF16)
        upd = _dot_tn(kd, vh.astype(BF16))
        a_col = jnp.exp2(jnp.transpose(b_scr[tm - 8:tm, hs])[:, 7:8])
        hst_ref[0, h] = jnp.where(live, a_col * s_prev + upd, s_prev)
        _finish_head(h, o, proj_scr, vec_ref, o_scr)

    fill.flush()
    y_ref[0] = _ffn_norm(x1_scr[...], acc_scr[...], vec_ref)
    xs_scr[...] = x
    mix_scr[:, 0:HGRN_WIDTH] = o_scr[...].astype(BF16)


def _sample_kernel(x_ref, sin_ref, ext_ref, win_ref, wout_ref, w1_ref, w2_ref, vec_ref, lv_ref, tri_ref,
                   y_ref, sout_ref, u_ref,
                   proj_scr, q_scr, kk_scr, lf_scr, b_scr, o_scr, qe_scr, kd_scr, vb_scr, ys_scr):
    seg = 8
    tm = SAMPLE_SEQS * seg
    rows = pl.ds(pl.multiple_of(pl.program_id(1) * tm, tm), tm)

    @pl.when(pl.program_id(0) == 0)
    def _():
        ys_scr[rows, :] = x_ref[...]

    x = ys_scr[rows, :]
    _front(x, win_ref, vec_ref, tri_ref, proj_scr, q_scr, kk_scr, lf_scr, b_scr)

    for h in range(0, N_HEADS, 2):
        pair = _intra_head_pair(h, seg, proj_scr, q_scr, kk_scr, lf_scr, b_scr, lv_ref)
        o_scr[:, h * HEAD_DIM:(h + 1) * HEAD_DIM] = pair[0]
        o_scr[:, (h + 1) * HEAD_DIM:(h + 2) * HEAD_DIM] = pair[1]
    for h in range(N_HEADS):
        hs = slice(h * HEAD_DIM, (h + 1) * HEAD_DIM)
        bh = b_scr[:, hs]
        pieces = [jnp.broadcast_to(b_scr[pl.ds(seg * jb + seg - 1, 1), hs], (seg, HEAD_DIM))
                  for jb in range(tm // seg)]
        b_last = jnp.concatenate(pieces, axis=0)
        qe_scr[:, hs] = (q_scr[:, hs] * jnp.exp2(bh)).astype(BF16)
        kd_scr[:, hs] = (kk_scr[:, hs] * jnp.exp2(b_last - bh)).astype(BF16)
    vb_scr[...] = proj_scr[:, OFF_I:OFF_I + HGRN_WIDTH].astype(BF16)

    top = lax.broadcasted_iota(jnp.int32, (2 * seg, HEAD_DIM), 0) < seg
    for h in range(N_HEADS):
        hs = slice(h * HEAD_DIM, (h + 1) * HEAD_DIM)
        b_last = jnp.concatenate([b_scr[r:r + 1, hs] for r in range(seg - 1, tm, seg)], axis=0)
        a_cols = jnp.exp2(jnp.transpose(b_last))
        for p in range(SAMPLE_SEQS // 2):
            rows16 = slice(p * 2 * seg, (p + 1) * 2 * seg)
            qe = qe_scr[rows16, hs]
            kd = kd_scr[rows16, hs]
            vv = vb_scr[rows16, hs]
            s_a = sin_ref[2 * p, h]
            s_b = sin_ref[2 * p + 1, h]
            o_ab = _dot(qe, jnp.concatenate([s_a, s_b], axis=1).astype(BF16))
            o_scr[rows16, hs] = o_scr[rows16, hs] + jnp.where(top, o_ab[:, 0:HEAD_DIM], o_ab[:, HEAD_DIM:2 * HEAD_DIM])
            zero = jnp.zeros_like(kd)
            upd = _dot_tn(jnp.concatenate([jnp.where(top, kd, zero), jnp.where(top, zero, kd)], axis=1), vv)
            sout_ref[2 * p, h] = a_cols[:, 2 * p:2 * p + 1] * s_a + upd[0:HEAD_DIM]
            sout_ref[2 * p + 1, h] = a_cols[:, 2 * p + 1:2 * p + 2] * s_b + upd[HEAD_DIM:2 * HEAD_DIM]

    for h in range(N_HEADS):
        hs = slice(h * HEAD_DIM, (h + 1) * HEAD_DIM)
        _finish_head(h, o_scr[:, hs], proj_scr, vec_ref, o_scr)

    u = proj_scr[:, OFF_C:OFF_C + CONV_WIDTH] * proj_scr[:, OFF_H:OFF_H + CONV_WIDTH]
    u_ref[...] = u
    ext = ext_ref[...]
    tmod = lax.broadcasted_iota(jnp.int32, (tm, CONV_WIDTH), 0) & (seg - 1)
    u1 = jnp.where(tmod == 0, pltpu.roll(ext, tm - 1, 0), pltpu.roll(u, 1, 0))
    u2 = jnp.where(tmod < 2, ext, pltpu.roll(u, 2, 0))
    w0, w1, w2 = _conv_weights(vec_ref)
    yc = proj_scr[:, OFF_B:OFF_B + CONV_WIDTH] * (w0 * u2 + w1 * u1 + w2 * u)

    mixin = jnp.concatenate([o_scr[...], yc], axis=-1).astype(BF16)
    y = _back(x, mixin, wout_ref, w1_ref, w2_ref, vec_ref)
    ys_scr[rows, :] = y
    y_ref[...] = y


def _level_table(tm, seg):
    t = np.arange(tm)[:, None]
    s = np.arange(tm)[None, :]
    x = np.maximum(t ^ s, 1)
    lv = np.floor(np.log2(x)).astype(np.int32)
    ok = (s < t) & ((t // seg) == (s // seg))
    return np.where(ok, lv, -1).astype(np.int32)


def _tri_table(tm, seg):
    t = np.arange(tm)[:, None]
    s = np.arange(tm)[None, :]
    return ((s <= t) & ((t // seg) == (s // seg))).astype(np.float32)


def _const_spec(shape):
    nd = len(shape)
    return pl.BlockSpec(shape, lambda *_: (0,) * nd, pipeline_mode=pl.Buffered(1))


def _weight_specs(layer_of):
    def wspec(k, n):
        return pl.BlockSpec((None, k, n), lambda *g: (layer_of(*g), 0, 0), pipeline_mode=pl.Buffered(1))
    return [wspec(D_MODEL, PROJ_WIDTH), wspec(D_MODEL, D_MODEL), wspec(D_MODEL, D_FF), wspec(D_FF, D_MODEL),
            wspec(8, D_MODEL)]


def _prompt_layer(layer, x, weights, vecs):
    bsz, seq, _ = x.shape
    tm = PROMPT_TILE
    tiles_per_seq = seq // tm
    n_tiles = bsz * tiles_per_seq
    lv = jnp.asarray(np.tile(_level_table(DIAG, DIAG), (1, SCORE_GROUP)))
    tri = jnp.asarray(_tri_table(tm, tm), dtype=BF16)

    def cur(i):
        t = jnp.minimum(i, n_tiles - 1)
        return t // tiles_per_seq, t % tiles_per_seq

    def prev(i):
        t = jnp.maximum(i - 1, 0)
        return t // tiles_per_seq, t % tiles_per_seq

    return pl.pallas_call(
        functools.partial(_prompt_kernel, n_tiles=n_tiles, tiles_per_seq=tiles_per_seq),
        grid=(n_tiles + 1,),
        in_specs=[pl.BlockSpec((1, tm, D_MODEL), lambda i: (*cur(i), 0))] + _weight_specs(lambda i: layer)
        + [_const_spec((DIAG, SCORE_GROUP * DIAG)), _const_spec((tm, tm))],
        out_specs=[pl.BlockSpec((1, tm, D_MODEL), lambda i: (*prev(i), 0)),
                   pl.BlockSpec((1, N_HEADS, HEAD_DIM, HEAD_DIM), lambda i: (cur(i)[0], 0, 0, 0)),
                   pl.BlockSpec((1, CONV_K - 1, CONV_WIDTH), lambda i: (cur(i)[0], 0, 0))],
        out_shape=[jax.ShapeDtypeStruct((bsz, seq, D_MODEL), F32),
                   jax.ShapeDtypeStruct((bsz, N_HEADS, HEAD_DIM, HEAD_DIM), F32),
                   jax.ShapeDtypeStruct((bsz, CONV_K - 1, CONV_WIDTH), F32)],
        scratch_shapes=[pltpu.VMEM((tm, PROJ_WIDTH), F32),
                        pltpu.VMEM((tm, HGRN_WIDTH), F32), pltpu.VMEM((tm, HGRN_WIDTH), F32),
                        pltpu.VMEM((tm, HGRN_WIDTH), F32), pltpu.VMEM((tm, HGRN_WIDTH), F32),
                        pltpu.VMEM((tm + 8, CONV_WIDTH), F32), pltpu.VMEM((tm, HGRN_WIDTH), F32),
                        pltpu.VMEM((tm, D_MODEL), F32), pltpu.VMEM((tm, D_MODEL), BF16),
                        pltpu.VMEM((tm, D_MODEL), F32), pltpu.VMEM((tm, D_MODEL), BF16),
                        pltpu.VMEM((tm, D_MODEL), F32), pltpu.VMEM((tm, FF_CHUNK), BF16)],
        compiler_params=pltpu.CompilerParams(dimension_semantics=("arbitrary",),
                                             vmem_limit_bytes=V7X_VMEM_LIMIT),
        name=f"prompt_layer{layer}",
    )(x, *weights, vecs, lv, tri)


def _sample_layers(x, state, ext, weights, vecs):
    rows = x.shape[0]
    seg = 8
    tm = SAMPLE_SEQS * seg
    n_seq = rows // seg
    n_steps = n_seq // SAMPLE_SEQS
    assert tm == DIAG
    lv = jnp.asarray(np.tile(_level_table(DIAG, seg), (1, 2)))
    tri = jnp.asarray(_tri_table(tm, seg), dtype=BF16)
    st_spec = pl.BlockSpec((None, SAMPLE_SEQS, N_HEADS, HEAD_DIM, HEAD_DIM), lambda l, i: (l, i, 0, 0, 0))
    u_spec = pl.BlockSpec((None, tm, CONV_WIDTH), lambda l, i: (l, i, 0))
    return pl.pallas_call(
        _sample_kernel,
        grid=(DEPTH, n_steps),
        in_specs=[pl.BlockSpec((tm, D_MODEL), lambda l, i: (i * (1 - l) + (n_steps - 1) * l, 0)), st_spec, u_spec]
        + _weight_specs(lambda l, i: l) + [_const_spec((DIAG, 2 * DIAG)), _const_spec((tm, tm))],
        out_specs=[pl.BlockSpec((tm, D_MODEL), lambda l, i: (i * l, 0)), st_spec, u_spec],
        out_shape=[jax.ShapeDtypeStruct((rows, D_MODEL), F32),
                   jax.ShapeDtypeStruct((DEPTH, n_seq, N_HEADS, HEAD_DIM, HEAD_DIM), F32),
                   jax.ShapeDtypeStruct((DEPTH, rows, CONV_WIDTH), F32)],
        scratch_shapes=[pltpu.VMEM((tm, PROJ_WIDTH), F32),
                        pltpu.VMEM((tm, HGRN_WIDTH), F32), pltpu.VMEM((tm, HGRN_WIDTH), F32),
                        pltpu.VMEM((tm, HGRN_WIDTH), F32), pltpu.VMEM((tm, HGRN_WIDTH), F32),
                        pltpu.VMEM((tm, HGRN_WIDTH), F32),
                        pltpu.VMEM((tm, HGRN_WIDTH), BF16), pltpu.VMEM((tm, HGRN_WIDTH), BF16),
                        pltpu.VMEM((tm, HGRN_WIDTH), BF16), pltpu.VMEM((rows, D_MODEL), F32)],
        compiler_params=pltpu.CompilerParams(dimension_semantics=("arbitrary", "arbitrary"),
                                             vmem_limit_bytes=V7X_VMEM_LIMIT),
        name="sample_layers",
    )(x, state, ext, *weights, vecs, lv, tri)


def kernel(x_prompt, x_sample, state_hgrn, state_conv, w_in, lb_logits, conv_w, onorm_g, w_out,
           ln1_g, ln1_b, w_ff1, w_ff2, ln2_g, ln2_b):
    n_seq, dec_seq, _ = x_sample.shape
    p = jax.nn.softmax(lb_logits.astype(F32), axis=0)
    cum = jnp.cumsum(p, axis=0)
    lb = cum - cum[0:1]
    zeros_half = jnp.zeros((DEPTH, CONV_WIDTH), F32)
    vecs = jnp.stack([
        ln1_g, ln1_b, ln2_g, ln2_b,
        jnp.concatenate([jnp.log(lb), jnp.log1p(-lb)], axis=-1),
        jnp.concatenate([1.0 - lb, onorm_g], axis=-1),
        jnp.concatenate([conv_w[:, 0], conv_w[:, 1]], axis=-1),
        jnp.concatenate([conv_w[:, 2], zeros_half], axis=-1)], axis=1).astype(F32)
    weights = (w_in.astype(BF16), w_out.astype(BF16), w_ff1.astype(BF16), w_ff2.astype(BF16))
    ext = jnp.pad(state_conv, ((0, 0), (0, 0), (0, dec_seq - (CONV_K - 1)), (0, 0)))
    ext = ext.reshape(DEPTH, n_seq * dec_seq, CONV_WIDTH)

    yp = x_prompt
    ys = x_sample.reshape(n_seq * dec_seq, D_MODEL)
    hp, cp = [], []
    for layer in range(DEPTH):
        yp, hgrn_p, conv_p = _prompt_layer(layer, yp, weights, vecs)
        hp.append(hgrn_p)
        cp.append(conv_p)
    ys, hgrn_s, u_s = _sample_layers(ys, state_hgrn, ext, weights, vecs)
    conv_s = u_s.reshape(DEPTH, n_seq, dec_seq, CONV_WIDTH)[:, :, dec_seq - (CONV_K - 1):]
    return (yp, ys.reshape(n_seq, dec_seq, D_MODEL), jnp.stack(hp), jnp.stack(cp), hgrn_s, conv_s)
```
